```python
import math
import jax, jax.numpy as jnp
from jax import lax
import numpy as np

D_MODEL = 2048
BATCH = 1
SEQ = 16384
DEPTH = 4

N_MIXERS = 4
EPS = 1e-6
NEG_INF = -1e30
MEM_LEN = 256

A_HEADS = 32
A_KV_HEADS = 4
A_HEAD_DIM = 64
A_GROUP = A_HEADS // A_KV_HEADS
A_QKV = (A_HEADS + 2 * A_KV_HEADS) * A_HEAD_DIM
WINDOW = 128
A_BLOCK = 128
REL_BUCKETS = 32
REL_MAX_EXACT = 16
REL_MAX_DIST = 128

B_HEADS = 4
B_KEY_DIM = D_MODEL // 2 // B_HEADS
B_VAL_DIM = D_MODEL // B_HEADS
B_QKVR = 2 * B_HEADS * B_KEY_DIM + 2 * B_HEADS * B_VAL_DIM
B_GATE_RANK = 16
B_GATE_TAU = 16.0
B_CHUNK = 64

C_KERNEL = 31

D_CHUNK = 128
D_GROUPS = 8
D_HALF = 2 * D_MODEL

X_HEADS = 4
X_HEAD_DIM = 128

FFN_DIM = 4 * D_MODEL
FFN_KERNEL = 3

N_LAYERS_A = (DEPTH + 3) // N_MIXERS
N_LAYERS_B = (DEPTH + 2) // N_MIXERS
N_LAYERS_C = (DEPTH + 1) // N_MIXERS
N_LAYERS_D = DEPTH // N_MIXERS

kernel_name = 'hybrid_interleaved_swa_gla_conformer_gmlp_trunk'


def rmsnorm(x, g):
    xf = x.astype(jnp.float32)
    y = xf * lax.rsqrt(jnp.mean(xf * xf, axis=-1, keepdims=True) + EPS)
    return (y * g.astype(jnp.float32)).astype(x.dtype)


def layernorm(x, g, b):
    xf = x.astype(jnp.float32)
    mu = jnp.mean(xf, axis=-1, keepdims=True)
    xc = xf - mu
    var = jnp.mean(xc * xc, axis=-1, keepdims=True)
    y = xc * lax.rsqrt(var + EPS) * g.astype(jnp.float32) + b.astype(jnp.float32)
    return y.astype(x.dtype)


def causal_depthwise_conv(x, w, b):
    k = w.shape[0]
    y = lax.conv_general_dilated(
        x, w[:, None, :].astype(x.dtype), window_strides=(1,), padding=[(k - 1, 0)],
        dimension_numbers=('NWC', 'WIO', 'NWC'), feature_group_count=x.shape[-1])
    return y + b


def t5_bucket(dist):
    n = np.maximum(dist, 0)
    large = REL_MAX_EXACT + (np.log(np.maximum(n, 1) / REL_MAX_EXACT)
                             / math.log(REL_MAX_DIST / REL_MAX_EXACT)
                             * (REL_BUCKETS - REL_MAX_EXACT)).astype(np.int32)
    large = np.minimum(large, REL_BUCKETS - 1)
    return np.where(n < REL_MAX_EXACT, n, large).astype(np.int32)


def sliding_window_attention(h, w_qkv, sinks, w_o, rel_table):
    b, s, _ = h.shape
    nb = s // A_BLOCK
    q, k, v = jnp.split(h @ w_qkv, [A_HEADS * A_HEAD_DIM, (A_HEADS + A_KV_HEADS) * A_HEAD_DIM], axis=-1)
    q = q.reshape(b, nb, A_BLOCK, A_KV_HEADS, A_GROUP, A_HEAD_DIM) * (A_HEAD_DIM ** -0.5)
    pad = ((0, 0), (A_BLOCK, 0), (0, 0))
    kp = jnp.pad(k, pad).reshape(b, nb + 1, A_BLOCK, A_KV_HEADS, A_HEAD_DIM)
    vp = jnp.pad(v, pad).reshape(b, nb + 1, A_BLOCK, A_KV_HEADS, A_HEAD_DIM)
    kb = jnp.concatenate([kp[:, :-1], kp[:, 1:]], axis=2)
    vb = jnp.concatenate([vp[:, :-1], vp[:, 1:]], axis=2)
    logits = jnp.einsum('bnikgd,bnjkd->bnkgij', q, kb).astype(jnp.float32)
    qi = np.arange(A_BLOCK)[:, None]
    kj = np.arange(2 * A_BLOCK)[None, :]
    dist = qi + A_BLOCK - kj
    in_window = (dist >= 0) & (dist < WINDOW)
    key_valid = (np.arange(nb)[:, None] > 0) | (kj >= A_BLOCK)
    mask = in_window[None, :, :] & key_valid[:, None, :]
    bias = rel_table.astype(jnp.float32)[t5_bucket(dist)]
    bias = jnp.transpose(bias, (2, 0, 1)).reshape(A_KV_HEADS, A_GROUP, A_BLOCK, 2 * A_BLOCK)
    logits = jnp.where(mask[None, :, None, None], logits + bias, NEG_INF)
    sink = sinks.astype(jnp.float32).reshape(A_KV_HEADS, A_GROUP)[None, None, :, :, None, None]
    m = jnp.maximum(jnp.max(logits, axis=-1, keepdims=True), sink)
    p = jnp.exp(logits - m)
    denom = jnp.sum(p, axis=-1, keepdims=True) + jnp.exp(sink - m)
    probs = (p / denom).astype(h.dtype)
    o = jnp.einsum('bnkgij,bnjkd->bnikgd', probs, vb)
    return o.reshape(b, s, A_HEADS * A_HEAD_DIM) @ w_o


def gated_linear_attention(h, w_qkvr, w_g1, w_g2, g_bias, o_norm, w_o):
    b, s, _ = h.shape
    nc = s // B_CHUNK
    dk_all = B_HEADS * B_KEY_DIM
    dv_all = B_HEADS * B_VAL_DIM
    q, k, v, r = jnp.split(h @ w_qkvr, [dk_all, 2 * dk_all, 2 * dk_all + dv_all], axis=-1)
    gk = (h @ w_g1) @ w_g2 + g_bias
    log_a = jax.nn.log_sigmoid(gk.astype(jnp.float32)) / B_GATE_TAU

    def chunks(t, d):
        return t.astype(jnp.float32).reshape(b, nc, B_CHUNK, B_HEADS, d)

    q = chunks(q, B_KEY_DIM) * (B_KEY_DIM ** -0.5)
    k = chunks(k, B_KEY_DIM)
    v = chunks(v, B_VAL_DIM)
    cum = jnp.cumsum(chunks(log_a, B_KEY_DIM), axis=2)
    last = cum[:, :, -1]
    q_dec = q * jnp.exp(cum)
    k_inv = k * jnp.exp(-cum)
    k_end = k * jnp.exp(last[:, :, None] - cum)
    causal = np.tril(np.ones((B_CHUNK, B_CHUNK), dtype=bool))
    att = jnp.where(causal, jnp.einsum('bnihd,bnjhd->bnhij', q_dec, k_inv), 0.0)
    o_intra = jnp.einsum('bnhij,bnjhe->bnihe', att, v)

    def step(state, xs):
        q_c, k_c, v_c, last_c = xs
        o_c = jnp.einsum('bihd,bhde->bihe', q_c, state)
        state = jnp.exp(last_c)[..., None] * state + jnp.einsum('bjhd,bjhe->bhde', k_c, v_c)
        return state, o_c

    xs = (jnp.moveaxis(q_dec, 1, 0), jnp.moveaxis(k_end, 1, 0),
          jnp.moveaxis(v, 1, 0), jnp.moveaxis(last, 1, 0))
    state0 = jnp.zeros((b, B_HEADS, B_KEY_DIM, B_VAL_DIM), jnp.float32)
    _, o_inter = lax.scan(step, state0, xs)
    o = o_intra + jnp.moveaxis(o_inter, 0, 1)
    o = rmsnorm(o, o_norm).reshape(b, s, dv_all).astype(h.dtype)
    return (o * jax.nn.silu(r)) @ w_o


def conformer_conv_module(h, w_pw1, b_pw1, w_dw, b_dw, ln_g, ln_b, w_pw2, b_pw2):
    a, g = jnp.split(h @ w_pw1 + b_pw1, 2, axis=-1)
    z = causal_depthwise_conv(a * jax.nn.sigmoid(g), w_dw, b_dw)
    z = jax.nn.silu(layernorm(z, ln_g, ln_b))
    return z @ w_pw2 + b_pw2


def chunked_spatial_gating(h, w_in, b_in, ln_g, ln_b, w_s, b_s, w_out):
    b, s, _ = h.shape
    nch = s // D_CHUNK
    u, v = jnp.split(jax.nn.gelu(h @ w_in + b_in, approximate=False), 2, axis=-1)
    v = layernorm(v, ln_g, ln_b).reshape(b, nch, D_CHUNK, D_GROUPS, D_HALF // D_GROUPS)
    w = jnp.where(np.tril(np.ones((D_CHUNK, D_CHUNK), dtype=bool)), w_s, 0.0)
    sv = jnp.einsum('gts,bcsgd->bctgd', w, v) + b_s.T[:, :, None]
    return (u * sv.reshape(b, s, D_HALF)) @ w_out


def memory_cross_attention(h, mem_n, w_q, w_kv, w_o):
    b, s, _ = h.shape
    m = mem_n.shape[1]
    q = (h @ w_q).reshape(b, s, X_HEADS, X_HEAD_DIM) * (X_HEAD_DIM ** -0.5)
    k, v = jnp.split(mem_n @ w_kv, 2, axis=-1)
    k = k.reshape(b, m, X_HEADS, X_HEAD_DIM)
    v = v.reshape(b, m, X_HEADS, X_HEAD_DIM)
    logits = jnp.einsum('bshd,bmhd->bhsm', q, k).astype(jnp.float32)
    p = jax.nn.softmax(logits, axis=-1).astype(h.dtype)
    o = jnp.einsum('bhsm,bmhd->bshd', p, v).reshape(b, s, X_HEADS * X_HEAD_DIM)
    return o @ w_o


def conv_gated_ffn(h, w_gate_up, w_conv, b_conv, w_down):
    gate, up = jnp.split(h @ w_gate_up, 2, axis=-1)
    gate = causal_depthwise_conv(gate, w_conv, b_conv)
    return (jax.nn.gelu(gate, approximate=True) * up) @ w_down


def setup_inputs(seed: int = 0) -> dict:
    key = jax.random.key(seed)
    keys = iter(jax.random.split(key, 64))
    f32 = jnp.float32

    def wt(shape, fan_in):
        return jax.random.normal(next(keys), shape, f32) * (fan_in ** -0.5)

    def gain(shape):
        return 1.0 + 0.05 * jax.random.normal(next(keys), shape, f32)

    def small(shape, scale=0.02):
        return scale * jax.random.normal(next(keys), shape, f32)

    D = D_MODEL
    return {
        'x': jax.random.normal(next(keys), (BATCH, SEQ, D), f32),
        'mem': jax.random.normal(next(keys), (BATCH, MEM_LEN, D), f32),
        'norm_mix_pre': gain((DEPTH, D)),
        'norm_mix_post': gain((DEPTH, D)),
        'norm_mem': gain((DEPTH, D)),
        'norm_xattn_pre': gain((DEPTH, D)),
        'norm_xattn_post': gain((DEPTH, D)),
        'norm_ffn_pre': gain((DEPTH, D)),
        'norm_ffn_post': gain((DEPTH, D)),
        'rel_bias_table': small((REL_BUCKETS, A_HEADS), 0.5),
        'a_w_qkv': wt((N_LAYERS_A, D, A_QKV), D),
        'a_sinks': small((N_LAYERS_A, A_HEADS), 0.5),
        'a_w_o': wt((N_LAYERS_A, A_HEADS * A_HEAD_DIM, D), A_HEADS * A_HEAD_DIM),
        'b_w_qkvr': wt((N_LAYERS_B, D, B_QKVR), D),
        'b_w_gate1': wt((N_LAYERS_B, D, B_GATE_RANK), D),
        'b_w_gate2': wt((N_LAYERS_B, B_GATE_RANK, B_HEADS * B_KEY_DIM), B_GATE_RANK),
        'b_gate_bias': small((N_LAYERS_B, B_HEADS * B_KEY_DIM), 0.1),
        'b_o_norm': gain((N_LAYERS_B, B_VAL_DIM)),
        'b_w_o': wt((N_LAYERS_B, B_HEADS * B_VAL_DIM, D), B_HEADS * B_VAL_DIM),
        'c_w_pw1': wt((N_LAYERS_C, D, 2 * D), D),
        'c_b_pw1': small((N_LAYERS_C, 2 * D)),
        'c_w_dw': wt((N_LAYERS_C, C_KERNEL, D), C_KERNEL),
        'c_b_dw': small((N_LAYERS_C, D)),
        'c_ln_g': gain((N_LAYERS_C, D)),
        'c_ln_b': small((N_LAYERS_C, D)),
        'c_w_pw2': wt((N_LAYERS_C, D, D), D),
        'c_b_pw2': small((N_LAYERS_C, D)),
        'd_w_in': wt((N_LAYERS_D, D, 2 * D_HALF), D),
        'd_b_in': small((N_LAYERS_D, 2 * D_HALF)),
        'd_ln_g': gain((N_LAYERS_D, D_HALF)),
        'd_ln_b': small((N_LAYERS_D, D_HALF)),
        'd_w_s': wt((N_LAYERS_D, D_GROUPS, D_CHUNK, D_CHUNK), D_CHUNK),
        'd_b_s': gain((N_LAYERS_D, D_GROUPS, D_CHUNK)),
        'd_w_out': wt((N_LAYERS_D, D_HALF, D), D_HALF),
        'x_w_q': wt((DEPTH, D, X_HEADS * X_HEAD_DIM), D),
        'x_w_kv': wt((DEPTH, D, 2 * X_HEADS * X_HEAD_DIM), D),
        'x_w_o': wt((DEPTH, X_HEADS * X_HEAD_DIM, D), X_HEADS * X_HEAD_DIM),
        'f_w_gate_up': wt((DEPTH, D, 2 * FFN_DIM), D),
        'f_w_conv': wt((DEPTH, FFN_KERNEL, FFN_DIM), FFN_KERNEL),
        'f_b_conv': small((DEPTH, FFN_DIM)),
        'f_w_down': wt((DEPTH, FFN_DIM, D), FFN_DIM),
    }


def reference(x, mem, norm_mix_pre, norm_mix_post, norm_mem, norm_xattn_pre, norm_xattn_post,
              norm_ffn_pre, norm_ffn_post, rel_bias_table,
              a_w_qkv, a_sinks, a_w_o,
              b_w_qkvr, b_w_gate1, b_w_gate2, b_gate_bias, b_o_norm, b_w_o,
              c_w_pw1, c_b_pw1, c_w_dw, c_b_dw, c_ln_g, c_ln_b, c_w_pw2, c_b_pw2,
              d_w_in, d_b_in, d_ln_g, d_ln_b, d_w_s, d_b_s, d_w_out,
              x_w_q, x_w_kv, x_w_o,
              f_w_gate_up, f_w_conv, f_b_conv, f_w_down):
    for i in range(DEPTH):
        kind, j = i % N_MIXERS, i // N_MIXERS
        h = rmsnorm(x, norm_mix_pre[i])
        if kind == 0:
            y = sliding_window_attention(h, a_w_qkv[j], a_sinks[j], a_w_o[j], rel_bias_table)
        elif kind == 1:
            y = gated_linear_attention(h, b_w_qkvr[j], b_w_gate1[j], b_w_gate2[j],
                                       b_gate_bias[j], b_o_norm[j], b_w_o[j])
        elif kind == 2:
            y = conformer_conv_module(h, c_w_pw1[j], c_b_pw1[j], c_w_dw[j], c_b_dw[j],
                                      c_ln_g[j], c_ln_b[j], c_w_pw2[j], c_b_pw2[j])
        else:
            y = chunked_spatial_gating(h, d_w_in[j], d_b_in[j], d_ln_g[j], d_ln_b[j],
                                       d_w_s[j], d_b_s[j], d_w_out[j])
        x = x + rmsnorm(y, norm_mix_post[i])
        mem_n = rmsnorm(mem, norm_mem[i])
        y = memory_cross_attention(rmsnorm(x, norm_xattn_pre[i]), mem_n, x_w_q[i], x_w_kv[i], x_w_o[i])
        x = x + rmsnorm(y, norm_xattn_post[i])
        y = conv_gated_ffn(rmsnorm(x, norm_ffn_pre[i]), f_w_gate_up[i], f_w_conv[i], f_b_conv[i], f_w_down[i])
        x = x + rmsnorm(y, norm_ffn_post[i])
    return x
```

```python
import functools
import math

import numpy as np
import jax
import jax.numpy as jnp
from jax import lax
from jax.experimental import pallas as pl
from jax.experimental.pallas import tpu as pltpu

D_MODEL = 2048
DEPTH = 4
EPS = 1e-6
NEG_INF = -1e30

A_HEADS = 32
A_KV_HEADS = 4
A_HEAD_DIM = 64
A_BLOCK = 128
REL_BUCKETS = 32
REL_MAX_EXACT = 16
REL_MAX_DIST = 128

B_HEADS = 4
B_KEY_DIM = 256
B_VAL_DIM = 512
B_GATE_RANK = 16
B_GATE_TAU = 16.0
B_CHUNK = 64

C_KERNEL = 31
C_HALO = 32

D_CHUNK = 128
D_GROUPS = 8
D_HALF = 2 * D_MODEL

X_HEADS = 4
X_HEAD_DIM = 128

FFN_DIM = 4 * D_MODEL
FFN_KERNEL = 3
FFN_HALO = 8

LANES = 128
MIB = 1024 * 1024

BF16 = jnp.bfloat16
F32 = jnp.float32


def _params(semantics, vmem_mib):
    return pltpu.CompilerParams(dimension_semantics=semantics, vmem_limit_bytes=vmem_mib * MIB)


def _dot(a, b):
    return jnp.dot(a, b, preferred_element_type=F32)


def _dot_nt(a, b):
    return lax.dot_general(a, b, (((1,), (1,)), ((), ())), preferred_element_type=F32)


def _dot_tn(a, b):
    return lax.dot_general(a, b, (((0,), (0,)), ((), ())), preferred_element_type=F32)


def _rms(y):
    return y * lax.rsqrt(jnp.mean(y * y, axis=-1, keepdims=True) + EPS)


def _residual_norm(y, x, g_post, g_next):
    xn = x + _rms(y) * g_post
    if g_next is None:
        return xn, None
    return xn, (_rms(xn) * g_next).astype(BF16)


def _gelu_tanh(x):
    return 0.5 * x * (1.0 + jnp.tanh(math.sqrt(2.0 / math.pi) * (x + 0.044715 * (x * x * x))))


def _gelu_erf(x):
    return 0.5 * x * (1.0 + lax.erf(x * math.sqrt(0.5)))


def _silu(x):
    return x * jax.nn.sigmoid(x)


def _row(v):
    return v.reshape(1, -1).astype(F32)


def _rmsnorm_kernel(x_ref, g_ref, o_ref):
    o_ref[...] = (_rms(x_ref[...]) * g_ref[...]).astype(o_ref.dtype)


def _rmsnorm(x, g, tm=512):
    s, d = x.shape
    return pl.pallas_call(
        _rmsnorm_kernel,
        grid=(s // tm,),
        in_specs=[pl.BlockSpec((tm, d), lambda i: (i, 0)), pl.BlockSpec((1, d), lambda i: (0, 0))],
        out_specs=pl.BlockSpec((tm, d), lambda i: (i, 0)),
        out_shape=jax.ShapeDtypeStruct((s, d), BF16),
        compiler_params=_params(("arbitrary",), 32),
        name="rmsnorm",
    )(x, _row(g))


def _proj_plain_kernel(h_ref, w_ref, o_ref):
    o_ref[...] = _dot(h_ref[...], w_ref[...]).astype(o_ref.dtype)


def _proj_gelu_kernel(h_ref, w_ref, b_ref, o_ref):
    o_ref[...] = _gelu_erf(_dot(h_ref[...], w_ref[...]) + b_ref[...]).astype(o_ref.dtype)


def _proj_glu_kernel(h_ref, wa_ref, wg_ref, ba_ref, bg_ref, o_ref):
    h = h_ref[...]
    a = _dot(h, wa_ref[...]) + ba_ref[...]
    g = _dot(h, wg_ref[...]) + bg_ref[...]
    o_ref[...] = (a * jax.nn.sigmoid(g)).astype(o_ref.dtype)


def _proj(h, w, out_dtype, tm=1024, tn=1024, name="proj"):
    s, k = h.shape
    n = w.shape[1]
    tn = min(tn, n)
    return pl.pallas_call(
        _proj_plain_kernel,
        grid=(s // tm, n // tn),
        in_specs=[pl.BlockSpec((tm, k), lambda i, j: (i, 0)), pl.BlockSpec((k, tn), lambda i, j: (0, j))],
        out_specs=pl.BlockSpec((tm, tn), lambda i, j: (i, j)),
        out_shape=jax.ShapeDtypeStruct((s, n), out_dtype),
        compiler_params=_params(("arbitrary", "arbitrary"), 48),
        name=name,
    )(h, w)


def _proj_gelu(h, w, b, out_dtype, tm=1024, tn=1024, name="proj_gelu"):
    s, k = h.shape
    n = w.shape[1]
    return pl.pallas_call(
        _proj_gelu_kernel,
        grid=(s // tm, n // tn),
        in_specs=[pl.BlockSpec((tm, k), lambda i, j: (i, 0)), pl.BlockSpec((k, tn), lambda i, j: (0, j)),
                  pl.BlockSpec((1, tn), lambda i, j: (0, j))],
        out_specs=pl.BlockSpec((tm, tn), lambda i, j: (i, j)),
        out_shape=jax.ShapeDtypeStruct((s, n), out_dtype),
        compiler_params=_params(("arbitrary", "arbitrary"), 48),
        name=name,
    )(h, w, _row(b))


def _proj_glu(h, w, b, tm=1024, tn=512, name="proj_glu"):
    s, k = h.shape
    n = w.shape[1] // 2
    nb = n // tn
    b2 = _row(b)
    return pl.pallas_call(
        _proj_glu_kernel,
        grid=(s // tm, nb),
        in_specs=[pl.BlockSpec((tm, k), lambda i, j: (i, 0)),
                  pl.BlockSpec((k, tn), lambda i, j: (0, j)),
                  pl.BlockSpec((k, tn), lambda i, j: (0, j + nb)),
                  pl.BlockSpec((1, tn), lambda i, j: (0, j)),
                  pl.BlockSpec((1, tn), lambda i, j: (0, j + nb))],
        out_specs=pl.BlockSpec((tm, tn), lambda i, j: (i, j)),
        out_shape=jax.ShapeDtypeStruct((s, n), F32),
        compiler_params=_params(("arbitrary", "arbitrary"), 48),
        name=name,
    )(h, w, w, b2, b2)


def _out_proj_kernel(a_ref, w_ref, b_ref, x_ref, gp_ref, gn_ref, xo_ref, ho_ref):
    y = _dot(a_ref[...], w_ref[...]) + b_ref[...]
    xn, hn = _residual_norm(y, x_ref[...], gp_ref[...], gn_ref[...])
    xo_ref[...] = xn
    ho_ref[...] = hn


def _out_proj(a, w, b, x, g_post, g_next, tm=512, name="out_proj"):
    s, k = a.shape
    d = w.shape[1]
    row = pl.BlockSpec((1, d), lambda i: (0, 0))
    return pl.pallas_call(
        _out_proj_kernel,
        grid=(s // tm,),
        in_specs=[pl.BlockSpec((tm, k), lambda i: (i, 0)), pl.BlockSpec((k, d), lambda i: (0, 0)), row,
                  pl.BlockSpec((tm, d), lambda i: (i, 0)), row, row],
        out_specs=[pl.BlockSpec((tm, d), lambda i: (i, 0)), pl.BlockSpec((tm, d), lambda i: (i, 0))],
        out_shape=[jax.ShapeDtypeStruct((s, d), F32), jax.ShapeDtypeStruct((s, d), BF16)],
        compiler_params=_params(("arbitrary",), 56),
        name=name,
    )(a, w, _row(b), x, _row(g_post), _row(g_next))


def _t5_bucket(dist):
    n = np.maximum(dist, 0)
    large = REL_MAX_EXACT + (np.log(np.maximum(n, 1) / REL_MAX_EXACT)
                             / math.log(REL_MAX_DIST / REL_MAX_EXACT)
                             * (REL_BUCKETS - REL_MAX_EXACT)).astype(np.int32)
    large = np.minimum(large, REL_BUCKETS - 1)
    return np.where(n < REL_MAX_EXACT, n, large).astype(np.int32)


def _swa_geometry():
    qi = np.arange(A_BLOCK)[:, None]
    kj = np.arange(2 * A_BLOCK)[None, :]
    dist = qi + A_BLOCK - kj
    in_window = ((dist >= 0) & (dist < A_BLOCK)).astype(np.float32)
    return _t5_bucket(dist), np.concatenate([in_window, in_window], axis=1)


def _swa_bias_kernel(tab_ref, bucket_ref, o_ref):
    pair = pl.program_id(0)
    bucket = bucket_ref[...]
    for half in range(2):
        head = 2 * pair + half
        acc = jnp.zeros(bucket.shape, F32)
        for b in range(REL_BUCKETS):
            acc = jnp.where(bucket == b, tab_ref[b, head], acc)
        o_ref[0, :, half * 2 * A_BLOCK:(half + 1) * 2 * A_BLOCK] = acc


def _swa_bias(rel_table, bucket):
    return pl.pallas_call(
        _swa_bias_kernel,
        grid=(A_HEADS // 2,),
        in_specs=[pl.BlockSpec(memory_space=pltpu.SMEM),
                  pl.BlockSpec((A_BLOCK, 2 * A_BLOCK), lambda p: (0, 0))],
        out_specs=pl.BlockSpec((1, A_BLOCK, 4 * A_BLOCK), lambda p: (p, 0, 0)),
        out_shape=jax.ShapeDtypeStruct((A_HEADS // 2, A_BLOCK, 4 * A_BLOCK), F32),
        compiler_params=_params(("arbitrary",), 16),
        name="swa_bias",
    )(rel_table.astype(F32), jnp.asarray(bucket))


def _swa_kernel(sink_ref, q_ref, kvp_ref, kvc_ref, bias_ref, win_ref, o_ref):
    n = pl.program_id(0)
    hd = A_HEAD_DIM
    nk = 2 * A_BLOCK
    lane_k = lax.broadcasted_iota(jnp.int32, (nk, 2 * hd), 1)
    first_k = lane_k < hd
    col = lax.broadcasted_iota(jnp.int32, (A_BLOCK, 2 * nk), 1)
    key_valid = (n > 0) | ((col & (nk - 1)) >= A_BLOCK)
    mask = (win_ref[...] > 0.0) & key_valid
    lane_o = lax.broadcasted_iota(jnp.int32, (A_BLOCK, 2 * hd), 1)
    first_o = lane_o < hd
    pairs_per_kv = A_HEADS // A_KV_HEADS // 2
    kv_cols = A_KV_HEADS * 2 * hd
    for kvh in range(A_KV_HEADS):
        ks = slice(kvh * 2 * hd, (kvh + 1) * 2 * hd)
        vs = slice(kv_cols + kvh * 2 * hd, kv_cols + (kvh + 1) * 2 * hd)
        kk = jnp.concatenate([kvp_ref[:, ks], kvc_ref[:, ks]], axis=0)
        vv = jnp.concatenate([kvp_ref[:, vs], kvc_ref[:, vs]], axis=0)
        zero = jnp.zeros_like(kk)
        k_diag = jnp.concatenate([jnp.where(first_k, kk, zero), jnp.where(first_k, zero, kk)], axis=0)
        v_diag = jnp.concatenate([jnp.where(first_k, vv, zero), jnp.where(first_k, zero, vv)], axis=0)
        for j in range(pairs_per_kv):
            p = kvh * pairs_per_kv + j
            cols = slice(p * 2 * hd, (p + 1) * 2 * hd)
            q2 = q_ref[:, cols] * (hd ** -0.5)
            logits = _dot_nt(q2, k_diag)
            logits = jnp.where(mask, logits + bias_ref[p], NEG_INF)
            es, invs = [], []
            for half in range(2):
                lg = logits[:, half * nk:(half + 1) * nk]
                sink = sink_ref[2 * p + half]
                m = jnp.maximum(jnp.max(lg, axis=-1, keepdims=True), sink)
                e = jnp.exp(lg - m)
                den = jnp.sum(e, axis=-1, keepdims=True) + jnp.exp(sink - m)
                es.append(e.astype(BF16))
                invs.append(1.0 / den)
            o2 = _dot(jnp.concatenate(es, axis=1), v_diag)
            o_ref[:, cols] = (o2 * jnp.where(first_o, invs[0], invs[1])).astype(o_ref.dtype)


def _swa_attention(qkv, sinks, bias, window):
    s = qkv.shape[0]
    nq = A_HEADS * A_HEAD_DIM
    nkv = 2 * A_KV_HEADS * 2 * A_HEAD_DIM
    kv_blk = nq // nkv
    return pl.pallas_call(
        _swa_kernel,
        grid=(s // A_BLOCK,),
        in_specs=[pl.BlockSpec(memory_space=pltpu.SMEM),
                  pl.BlockSpec((A_BLOCK, nq), lambda n: (n, 0)),
                  pl.BlockSpec((A_BLOCK, nkv), lambda n: (jnp.maximum(n - 1, 0), kv_blk)),
                  pl.BlockSpec((A_BLOCK, nkv), lambda n: (n, kv_blk)),
                  pl.BlockSpec((A_HEADS // 2, A_BLOCK, 4 * A_BLOCK), lambda n: (0, 0, 0)),
                  pl.BlockSpec((A_BLOCK, 4 * A_BLOCK), lambda n: (0, 0))],
        out_specs=pl.BlockSpec((A_BLOCK, nq), lambda n: (n, 0)),
        out_shape=jax.ShapeDtypeStruct((s, nq), BF16),
        compiler_params=_params(("arbitrary",), 32),
        name="swa_attention",
    )(sinks.astype(F32), qkv, qkv, qkv, bias, window)


def _swa_layer(h, x, w_qkv, sinks, w_o, rel_table, g_post, g_next):
    nq = A_HEADS * A_HEAD_DIM
    nkv = A_KV_HEADS * A_HEAD_DIM
    wq = w_qkv[:, :nq]
    wk = w_qkv[:, nq:nq + nkv].reshape(D_MODEL, A_KV_HEADS, 1, A_HEAD_DIM)
    wv = w_qkv[:, nq + nkv:].reshape(D_MODEL, A_KV_HEADS, 1, A_HEAD_DIM)
    dup = lambda w: jnp.broadcast_to(w, (D_MODEL, A_KV_HEADS, 2, A_HEAD_DIM)).reshape(D_MODEL, 2 * nkv)
    w_ext = jnp.concatenate([wq, dup(wk), dup(wv)], axis=1).astype(BF16)
    qkv = _proj(h, w_ext, BF16, name="swa_qkv")
    bucket, window = _swa_geometry()
    bias = _swa_bias(rel_table, bucket)
    o = _swa_attention(qkv, sinks, bias, jnp.asarray(window))
    return _out_proj(o, w_o.astype(BF16), jnp.zeros((D_MODEL,), F32), x, g_post, g_next, name="swa_out")


def _gla_gate_kernel(h_ref, w1_ref, w2_ref, b_ref, o_ref):
    t = _dot(h_ref[...], w1_ref[...]).astype(BF16)
    gk = _dot(t, w2_ref[...]) + b_ref[...]
    log_sig = -(jnp.maximum(-gk, 0.0) + jnp.log1p(jnp.exp(-jnp.abs(gk))))
    o_ref[...] = log_sig / B_GATE_TAU


def _gla_gate(h, w1, w2, b, tm=512):
    s, d = h.shape
    n = w2.shape[1]
    w1p = jnp.zeros((d, LANES), BF16).at[:, :B_GATE_RANK].set(w1.astype(BF16))
    w2p = jnp.zeros((LANES, n), BF16).at[:B_GATE_RANK, :].set(w2.astype(BF16))
    return pl.pallas_call(
        _gla_gate_kernel,
        grid=(s // tm,),
        in_specs=[pl.BlockSpec((tm, d), lambda i: (i, 0)), pl.BlockSpec((d, LANES), lambda i: (0, 0)),
                  pl.BlockSpec((LANES, n), lambda i: (0, 0)), pl.BlockSpec((1, n), lambda i: (0, 0))],
        out_specs=pl.BlockSpec((tm, n), lambda i: (i, 0)),
        out_shape=jax.ShapeDtypeStruct((s, n), F32),
        compiler_params=_params(("arbitrary",), 32),
        name="gla_gate",
    )(h, w1p, w2p, _row(b))


def _gla_kernel(qkvr_ref, la_ref, onorm_ref, o_ref, st_ref, *, n_sub):
    @pl.when(pl.program_id(0) == 0)
    def _():
        st_ref[...] = jnp.zeros_like(st_ref)

    c = B_CHUNK
    dk_all = B_HEADS * B_KEY_DIM
    dv_all = B_HEADS * B_VAL_DIM
    causal = (lax.broadcasted_iota(jnp.int32, (c, c), 0) >= lax.broadcasted_iota(jnp.int32, (c, c), 1))
    tri = causal.astype(BF16)

    def chunk(ci, carry):
        rows = pl.ds(pl.multiple_of(ci * c, c), c)
        la = la_ref[rows, :]
        la_hi = la.astype(BF16)
        la_lo = (la - la_hi.astype(F32)).astype(BF16)
        cum = _dot(tri, la_hi) + _dot(tri, la_lo)
        last = cum[c - 1:c, :]
        q = qkvr_ref[rows, 0:dk_all].astype(F32) * (B_KEY_DIM ** -0.5)
        k = qkvr_ref[rows, dk_all:2 * dk_all].astype(F32)
        q_dec = (q * jnp.exp(cum)).astype(BF16)
        k_inv = (k * jnp.exp(-cum)).astype(BF16)
        k_end = (k * jnp.exp(last - cum)).astype(BF16)
        decay = jnp.exp(last)
        for hh in range(B_HEADS):
            ks = slice(hh * B_KEY_DIM, (hh + 1) * B_KEY_DIM)
            v = qkvr_ref[rows, 2 * dk_all + hh * B_VAL_DIM:2 * dk_all + (hh + 1) * B_VAL_DIM]
            r = qkvr_ref[rows, 2 * dk_all + dv_all + hh * B_VAL_DIM:
                         2 * dk_all + dv_all + (hh + 1) * B_VAL_DIM].astype(F32)
            att = jnp.where(causal, _dot_nt(q_dec[:, ks], k_inv[:, ks]), 0.0).astype(BF16)
            st = st_ref[hh]
            o = _dot(att, v) + _dot_nt(q_dec[:, ks], st.astype(BF16))
            st_ref[hh] = st * decay[:, ks] + _dot_tn(v, k_end[:, ks])
            o = _rms(o) * onorm_ref[...]
            o_ref[rows, hh * B_VAL_DIM:(hh + 1) * B_VAL_DIM] = (o * _silu(r)).astype(o_ref.dtype)
        return carry

    lax.fori_loop(0, n_sub, chunk, 0)


def _gla_core(qkvr, log_a, o_norm, tb=256):
    s = qkvr.shape[0]
    dv_all = B_HEADS * B_VAL_DIM
    return pl.pallas_call(
        functools.partial(_gla_kernel, n_sub=tb // B_CHUNK),
        grid=(s // tb,),
        in_specs=[pl.BlockSpec((tb, qkvr.shape[1]), lambda i: (i, 0)),
                  pl.BlockSpec((tb, log_a.shape[1]), lambda i: (i, 0)),
                  pl.BlockSpec((1, B_VAL_DIM), lambda i: (0, 0))],
        out_specs=pl.BlockSpec((tb, dv_all), lambda i: (i, 0)),
        out_shape=jax.ShapeDtypeStruct((s, dv_all), BF16),
        scratch_shapes=[pltpu.VMEM((B_HEADS, B_VAL_DIM, B_KEY_DIM), F32)],
        compiler_params=_params(("arbitrary",), 32),
        name="gla_core",
    )(qkvr, log_a, _row(o_norm))


def _gla_layer(h, x, w_qkvr, w_g1, w_g2, g_bias, o_norm, w_o, g_post, g_next):
    qkvr = _proj(h, w_qkvr.astype(BF16), BF16, name="gla_qkvr")
    log_a = _gla_gate(h, w_g1, w_g2, g_bias)
    o = _gla_core(qkvr, log_a, o_norm)
    return _out_proj(o, w_o.astype(BF16), jnp.zeros((D_MODEL,), F32), x, g_post, g_next, name="gla_out")


def _dwconv_ln_kernel(u_ref, uprev_ref, w_ref, b_ref, lng_ref, lnb_ref, o_ref, ext_ref, z_ref, *, tm, rc, cw):
    i = pl.program_id(0)
    ext_ref[0:C_HALO, :] = jnp.where(i > 0, uprev_ref[...], 0.0)
    ext_ref[C_HALO:C_HALO + tm, :] = u_ref[...]
    d = u_ref.shape[1]
    first = C_HALO - (C_KERNEL - 1)
    for r0 in range(0, tm, rc):
        for c0 in range(0, d, cw):
            cols = slice(c0, c0 + cw)
            acc = jnp.zeros((rc, cw), F32) + b_ref[:, cols]
            for k in range(C_KERNEL):
                acc = acc + w_ref[k:k + 1, cols] * ext_ref[r0 + first + k:r0 + first + k + rc, cols]
            z_ref[r0:r0 + rc, cols] = acc
    z = z_ref[...]
    zc = z - jnp.mean(z, axis=-1, keepdims=True)
    zn = zc * lax.rsqrt(jnp.mean(zc * zc, axis=-1, keepdims=True) + EPS) * lng_ref[...] + lnb_ref[...]
    o_ref[...] = _silu(zn).astype(o_ref.dtype)


def _dwconv_ln(u, w_dw, b_dw, ln_g, ln_b, tm=128, rc=64, cw=512):
    s, d = u.shape
    row = pl.BlockSpec((1, d), lambda i: (0, 0))
    halo_blocks = tm // C_HALO
    return pl.pallas_call(
        functools.partial(_dwconv_ln_kernel, tm=tm, rc=rc, cw=cw),
        grid=(s // tm,),
        in_specs=[pl.BlockSpec((tm, d), lambda i: (i, 0)),
                  pl.BlockSpec((C_HALO, d), lambda i: (jnp.maximum(i * halo_blocks - 1, 0), 0)),
                  pl.BlockSpec((C_KERNEL, d), lambda i: (0, 0)), row, row, row],
        out_specs=pl.BlockSpec((tm, d), lambda i: (i, 0)),
        out_shape=jax.ShapeDtypeStruct((s, d), BF16),
        scratch_shapes=[pltpu.VMEM((C_HALO + tm, d), F32), pltpu.VMEM((tm, d), F32)],
        compiler_params=_params(("arbitrary",), 32),
        name="dwconv_ln",
    )(u, u, w_dw.astype(F32), _row(b_dw), _row(ln_g), _row(ln_b))


def _conformer_layer(h, x, w_pw1, b_pw1, w_dw, b_dw, ln_g, ln_b, w_pw2, b_pw2, g_post, g_next):
    u = _proj_glu(h, w_pw1.astype(BF16), b_pw1, name="conf_pw1")
    z = _dwconv_ln(u, w_dw, b_dw, ln_g, ln_b)
    return _out_proj(z, w_pw2.astype(BF16), b_pw2, x, g_post, g_next, name="conf_out")


def _sgu_kernel(u_ref, v_ref, ws_ref, bs_ref, lng_ref, lnb_ref, wout_ref, x_ref, gp_ref, gn_ref,
                xo_ref, ho_ref, p_ref, *, tm):
    v = v_ref[...]
    vc = v - jnp.mean(v, axis=-1, keepdims=True)
    vn = (vc * lax.rsqrt(jnp.mean(vc * vc, axis=-1, keepdims=True) + EPS) * lng_ref[...]
          + lnb_ref[...]).astype(BF16)
    t = D_CHUNK
    gw = D_HALF // D_GROUPS
    tril = lax.broadcasted_iota(jnp.int32, (t, t), 0) >= lax.broadcasted_iota(jnp.int32, (t, t), 1)
    for g in range(D_GROUPS):
        w = jnp.where(tril, ws_ref[g], jnp.zeros((t, t), BF16))
        cols = slice(g * gw, (g + 1) * gw)
        for c0 in range(0, tm, t):
            sv = _dot(w, vn[c0:c0 + t, cols]) + bs_ref[g]
            p_ref[c0:c0 + t, cols] = (u_ref[c0:c0 + t, cols].astype(F32) * sv).astype(BF16)
    y = _dot(p_ref[...], wout_ref[...])
    xn, hn = _residual_norm(y, x_ref[...], gp_ref[...], gn_ref[...])
    xo_ref[...] = xn
    ho_ref[...] = hn


def _sgu(u, v, w_s, b_s, ln_g, ln_b, w_out, x, g_post, g_next, tm=256):
    s, dh = u.shape
    d = w_out.shape[1]
    row = pl.BlockSpec((1, d), lambda i: (0, 0))
    rowh = pl.BlockSpec((1, dh), lambda i: (0, 0))
    return pl.pallas_call(
        functools.partial(_sgu_kernel, tm=tm),
        grid=(s // tm,),
        in_specs=[pl.BlockSpec((tm, dh), lambda i: (i, 0)), pl.BlockSpec((tm, dh), lambda i: (i, 0)),
                  pl.BlockSpec((D_GROUPS, D_CHUNK, D_CHUNK), lambda i: (0, 0, 0)),
                  pl.BlockSpec((D_GROUPS, D_CHUNK, 1), lambda i: (0, 0, 0)),
                  rowh, rowh,
                  pl.BlockSpec((dh, d), lambda i: (0, 0), pipeline_mode=pl.Buffered(1)),
                  pl.BlockSpec((tm, d), lambda i: (i, 0)), row, row],
        out_specs=[pl.BlockSpec((tm, d), lambda i: (i, 0)), pl.BlockSpec((tm, d), lambda i: (i, 0))],
        out_shape=[jax.ShapeDtypeStruct((s, d), F32), jax.ShapeDtypeStruct((s, d), BF16)],
        scratch_shapes=[pltpu.VMEM((tm, dh), BF16)],
        compiler_params=_params(("arbitrary",), 56),
        name="sgu",
    )(u, v, w_s.astype(BF16), b_s.astype(F32)[:, :, None], _row(ln_g), _row(ln_b), w_out.astype(BF16),
      x, _row(g_post), _row(g_next))


def _gmlp_layer(h, x, w_in, b_in, ln_g, ln_b, w_s, b_s, w_out, g_post, g_next):
    w_in = w_in.astype(BF16)
    u = _proj_gelu(h, w_in[:, :D_HALF], b_in[:D_HALF], BF16, name="gmlp_in_u")
    v = _proj_gelu(h, w_in[:, D_HALF:], b_in[D_HALF:], F32, name="gmlp_in_v")
    return _sgu(u, v, w_s, b_s, ln_g, ln_b, w_out, x, g_post, g_next)


def _mem_kv_kernel(mem_ref, g_ref, w_ref, o_ref):
    mem_n = (_rms(mem_ref[...]) * g_ref[...]).astype(BF16)
    o_ref[...] = _dot(mem_n, w_ref[...]).astype(o_ref.dtype)


def _mem_kv(mem, g, w_kv):
    m, d = mem.shape
    n = w_kv.shape[1]
    tn = n // 2
    return pl.pallas_call(
        _mem_kv_kernel,
        grid=(n // tn,),
        in_specs=[pl.BlockSpec((m, d), lambda j: (0, 0)), pl.BlockSpec((1, d), lambda j: (0, 0)),
                  pl.BlockSpec((d, tn), lambda j: (0, j))],
        out_specs=pl.BlockSpec((m, tn), lambda j: (0, j)),
        out_shape=jax.ShapeDtypeStruct((m, n), BF16),
        compiler_params=_params(("arbitrary",), 32),
        name="mem_kv",
    )(mem, _row(g), w_kv.astype(BF16))


def _xattn_kernel(h_ref, wq_ref, kv_ref, wo_ref, x_ref, gp_ref, gn_ref, xo_ref, ho_ref):
    hd = X_HEAD_DIM
    nq = X_HEADS * hd
    q = (_dot(h_ref[...], wq_ref[...]) * (hd ** -0.5)).astype(BF16)
    outs = []
    for hh in range(X_HEADS):
        cols = slice(hh * hd, (hh + 1) * hd)
        logits = _dot_nt(q[:, cols], kv_ref[:, cols])
        e = jnp.exp(logits - jnp.max(logits, axis=-1, keepdims=True))
        inv = 1.0 / jnp.sum(e, axis=-1, keepdims=True)
        o = _dot(e.astype(BF16), kv_ref[:, nq + hh * hd:nq + (hh + 1) * hd])
        outs.append((o * inv).astype(BF16))
    y = _dot(jnp.concatenate(outs, axis=1), wo_ref[...])
    xn, hn = _residual_norm(y, x_ref[...], gp_ref[...], gn_ref[...])
    xo_ref[...] = xn
    ho_ref[...] = hn


def _xattn(h, x, w_q, kv, w_o, g_post, g_next, tm=512):
    s, d = h.shape
    nq = w_q.shape[1]
    row = pl.BlockSpec((1, d), lambda i: (0, 0))
    return pl.pallas_call(
        _xattn_kernel,
        grid=(s // tm,),
        in_specs=[pl.BlockSpec((tm, d), lambda i: (i, 0)), pl.BlockSpec((d, nq), lambda i: (0, 0)),
                  pl.BlockSpec(kv.shape, lambda i: (0, 0)), pl.BlockSpec((nq, d), lambda i: (0, 0)),
                  pl.BlockSpec((tm, d), lambda i: (i, 0)), row, row],
        out_specs=[pl.BlockSpec((tm, d), lambda i: (i, 0)), pl.BlockSpec((tm, d), lambda i: (i, 0))],
        out_shape=[jax.ShapeDtypeStruct((s, d), F32), jax.ShapeDtypeStruct((s, d), BF16)],
        compiler_params=_params(("arbitrary",), 48),
        name="xattn",
    )(h, w_q.astype(BF16), kv, w_o.astype(BF16), x, _row(g_post), _row(g_next))


def _ffn_kernel(h_ref, wg_ref, wu_ref, wc_ref, bc_ref, wd_ref, x_ref, gp_ref, *rest, tm, emit_h):
    if emit_h:
        gn_ref, xo_ref, ho_ref, acc_ref, halo_ref, ext_ref = rest
    else:
        xo_ref, acc_ref, halo_ref, ext_ref = rest
    i = pl.program_id(0)
    f = pl.program_id(1)

    @pl.when(i == 0)
    def _():
        halo_ref[f] = jnp.zeros(halo_ref.shape[1:], F32)

    h = h_ref[...]
    gate = _dot(h, wg_ref[...])
    up = _dot(h, wu_ref[...])
    ext_ref[0:FFN_HALO, :] = halo_ref[f]
    ext_ref[FFN_HALO:FFN_HALO + tm, :] = gate
    halo_ref[f] = gate[tm - FFN_HALO:tm, :]
    conv = (wc_ref[2:3, :] * gate
            + wc_ref[1:2, :] * ext_ref[FFN_HALO - 1:FFN_HALO - 1 + tm, :]
            + wc_ref[0:1, :] * ext_ref[FFN_HALO - 2:FFN_HALO - 2 + tm, :]
            + bc_ref[...])
    act = (_gelu_tanh(conv) * up).astype(BF16)
    part = _dot(act, wd_ref[...])

    @pl.when(f == 0)
    def _():
        acc_ref[...] = part

    @pl.when(f > 0)
    def _():
        acc_ref[...] += part

    @pl.when(f == pl.num_programs(1) - 1)
    def _():
        xn, hn = _residual_norm(acc_ref[...], x_ref[...], gp_ref[...], gn_ref[...] if emit_h else None)
        xo_ref[...] = xn
        if emit_h:
            ho_ref[...] = hn


def _ffn(h, x, w_gate_up, w_conv, b_conv, w_down, g_post, g_next, tm=512, fc=512):
    s, d = h.shape
    ff = w_down.shape[0]
    nf = ff // fc
    emit_h = g_next is not None
    row = pl.BlockSpec((1, d), lambda i, f: (0, 0))
    tile = pl.BlockSpec((tm, d), lambda i, f: (i, 0))
    in_specs = [tile,
                pl.BlockSpec((d, fc), lambda i, f: (0, f)),
                pl.BlockSpec((d, fc), lambda i, f: (0, f + nf)),
                pl.BlockSpec((FFN_KERNEL, fc), lambda i, f: (0, f)),
                pl.BlockSpec((1, fc), lambda i, f: (0, f)),
                pl.BlockSpec((fc, d), lambda i, f: (f, 0)),
                tile, row]
    args = [h, w_gate_up, w_gate_up, w_conv.astype(F32), _row(b_conv), w_down, x, _row(g_post)]
    out_specs = [tile]
    out_shape = [jax.ShapeDtypeStruct((s, d), F32)]
    if emit_h:
        in_specs.append(row)
        args.append(_row(g_next))
        out_specs.append(tile)
        out_shape.append(jax.ShapeDtypeStruct((s, d), BF16))
    outs = pl.pallas_call(
        functools.partial(_ffn_kernel, tm=tm, emit_h=emit_h),
        grid=(s // tm, nf),
        in_specs=in_specs,
        out_specs=out_specs,
        out_shape=out_shape,
        scratch_shapes=[pltpu.VMEM((tm, d), F32), pltpu.VMEM((nf, FFN_HALO, fc), F32),
                        pltpu.VMEM((FFN_HALO + tm, fc), F32)],
        compiler_params=_params(("arbitrary", "arbitrary"), 56),
        name="ffn",
    )(*args)
    return (outs[0], outs[1]) if emit_h else (outs[0], None)


def kernel(x, mem, norm_mix_pre, norm_mix_post, norm_mem, norm_xattn_pre, norm_xattn_post, norm_ffn_pre, norm_ffn_post, rel_bias_table, a_w_qkv, a_sinks, a_w_o, b_w_qkvr, b_w_gate1, b_w_gate2, b_gate_bias, b_o_norm, b_w_o, c_w_pw1, c_b_pw1, c_w_dw, c_b_dw, c_ln_g, c_ln_b, c_w_pw2, c_b_pw2, d_w_in, d_b_in, d_ln_g, d_ln_b, d_w_s, d_b_s, d_w_out, x_w_q, x_w_kv, x_w_o, f_w_gate_up, f_w_conv, f_b_conv, f_w_down):
    assert x.shape[0] == 1 and mem.shape[0] == 1
    xs = x[0]
    mem2 = mem[0]
    h = _rmsnorm(xs, norm_mix_pre[0])
    for i in range(DEPTH):
        kind, j = i % 4, i // 4
        g_post, g_next = norm_mix_post[i], norm_xattn_pre[i]
        if kind == 0:
            xs, h = _swa_layer(h, xs, a_w_qkv[j], a_sinks[j], a_w_o[j], rel_bias_table, g_post, g_next)
        elif kind == 1:
            xs, h = _gla_layer(h, xs, b_w_qkvr[j], b_w_gate1[j], b_w_gate2[j], b_gate_bias[j], b_o_norm[j],
                               b_w_o[j], g_post, g_next)
        elif kind == 2:
            xs, h = _conformer_layer(h, xs, c_w_pw1[j], c_b_pw1[j], c_w_dw[j], c_b_dw[j], c_ln_g[j],
                                     c_ln_b[j], c_w_pw2[j], c_b_pw2[j], g_post, g_next)
        else:
            xs, h = _gmlp_layer(h, xs, d_w_in[j], d_b_in[j], d_ln_g[j], d_ln_b[j], d_w_s[j], d_b_s[j],
                                d_w_out[j], g_post, g_next)
        kv = _mem_kv(mem2, norm_mem[i], x_w_kv[i])
        xs, h = _xattn(h, xs, x_w_q[i], kv, x_w_o[i], norm_xattn_post[i], norm_ffn_pre[i])
        g_next = norm_mix_pre[i + 1] if i + 1 < DEPTH else None
        xs, h = _ffn(h, xs, f_w_gate_up[i].astype(BF16), f_w_conv[i], f_b_conv[i], f_w_down[i].astype(BF16),
                     norm_ffn_post[i], g_next)
    return xs[None]
```

```python
import functools
import math

import numpy as np
import jax
import jax.numpy as jnp
from jax import lax
from jax.experimental import pallas as pl
from jax.experimental.pallas import tpu as pltpu

D_MODEL = 2048
DEPTH = 4
EPS = 1e-6
NEG_INF = -1e30

A_HEADS = 32
A_KV_HEADS = 4
A_HEAD_DIM = 64
A_BLOCK = 128
REL_BUCKETS = 32
REL_MAX_EXACT = 16
REL_MAX_DIST = 128

B_HEADS = 4
B_KEY_DIM = 256
B_VAL_DIM = 512
B_GATE_RANK = 16
B_GATE_TAU = 16.0
B_CHUNK = 64

C_KERNEL = 31
C_HALO = 32

D_CHUNK = 128
D_GROUPS = 8
D_HALF = 2 * D_MODEL

X_HEADS = 4
X_HEAD_DIM = 128

FFN_DIM = 4 * D_MODEL
FFN_KERNEL = 3
FFN_HALO = 8

LANES = 128
MIB = 1024 * 1024

BF16 = jnp.bfloat16
F32 = jnp.float32


def _params(semantics, vmem_mib):
    return pltpu.CompilerParams(dimension_semantics=semantics, vmem_limit_bytes=vmem_mib * MIB)


def _dot(a, b):
    return jnp.dot(a, b, preferred_element_type=F32)


def _dot_nt(a, b):
    return lax.dot_general(a, b, (((1,), (1,)), ((), ())), preferred_element_type=F32)


def _dot_tn(a, b):
    return lax.dot_general(a, b, (((0,), (0,)), ((), ())), preferred_element_type=F32)


def _rms(y):
    return y * lax.rsqrt(jnp.mean(y * y, axis=-1, keepdims=True) + EPS)


def _residual_norm(y, x, g_post, g_next):
    xn = x + _rms(y) * g_post
    if g_next is None:
        return xn, None
    return xn, (_rms(xn) * g_next).astype(BF16)


def _gelu_tanh(x):
    return 0.5 * x * (1.0 + jnp.tanh(math.sqrt(2.0 / math.pi) * (x + 0.044715 * (x * x * x))))


def _gelu_erf(x):
    return 0.5 * x * (1.0 + lax.erf(x * math.sqrt(0.5)))


def _silu(x):
    return x * jax.nn.sigmoid(x)


def _row(v):
    return v.reshape(1, -1).astype(F32)


def _rmsnorm_kernel(x_ref, g_ref, o_ref):
    o_ref[...] = (_rms(x_ref[...]) * g_ref[...]).astype(o_ref.dtype)


def _rmsnorm(x, g, tm=512):
    s, d = x.shape
    return pl.pallas_call(
        _rmsnorm_kernel,
        grid=(s // tm,),
        in_specs=[pl.BlockSpec((tm, d), lambda i: (i, 0)), pl.BlockSpec((1, d), lambda i: (0, 0))],
        out_specs=pl.BlockSpec((tm, d), lambda i: (i, 0)),
        out_shape=jax.ShapeDtypeStruct((s, d), BF16),
        compiler_params=_params(("arbitrary",), 32),
        name="rmsnorm",
    )(x, _row(g))


def _proj_plain_kernel(h_ref, w_ref, o_ref):
    o_ref[...] = _dot(h_ref[...], w_ref[...]).astype(o_ref.dtype)


def _proj_gelu_kernel(h_ref, w_ref, b_ref, o_ref):
    o_ref[...] = _gelu_erf(_dot(h_ref[...], w_ref[...]) + b_ref[...]).astype(o_ref.dtype)


def _proj_glu_kernel(h_ref, wa_ref, wg_ref, ba_ref, bg_ref, o_ref):
    h = h_ref[...]
    a = _dot(h, wa_ref[...]) + ba_ref[...]
    g = _dot(h, wg_ref[...]) + bg_ref[...]
    o_ref[...] = (a * jax.nn.sigmoid(g)).astype(o_ref.dtype)


def _proj(h, w, out_dtype, tm=1024, tn=1024, name="proj"):
    s, k = h.shape
    n = w.shape[1]
    tn = min(tn, n)
    return pl.pallas_call(
        _proj_plain_kernel,
        grid=(s // tm, n // tn),
        in_specs=[pl.BlockSpec((tm, k), lambda i, j: (i, 0)), pl.BlockSpec((k, tn), lambda i, j: (0, j))],
        out_specs=pl.BlockSpec((tm, tn), lambda i, j: (i, j)),
        out_shape=jax.ShapeDtypeStruct((s, n), out_dtype),
        compiler_params=_params(("arbitrary", "arbitrary"), 48),
        name=name,
    )(h, w)


def _proj_gelu(h, w, b, out_dtype, tm=1024, tn=1024, name="proj_gelu"):
    s, k = h.shape
    n = w.shape[1]
    return pl.pallas_call(
        _proj_gelu_kernel,
        grid=(s // tm, n // tn),
        in_specs=[pl.BlockSpec((tm, k), lambda i, j: (i, 0)), pl.BlockSpec((k, tn), lambda i, j: (0, j)),
                  pl.BlockSpec((1, tn), lambda i, j: (0, j))],
        out_specs=pl.BlockSpec((tm, tn), lambda i, j: (i, j)),
        out_shape=jax.ShapeDtypeStruct((s, n), out_dtype),
        compiler_params=_params(("arbitrary", "arbitrary"), 48),
        name=name,
    )(h, w, _row(b))


def _proj_glu(h, w, b, tm=1024, tn=512, name="proj_glu"):
    s, k = h.shape
    n = w.shape[1] // 2
    nb = n // tn
    b2 = _row(b)
    return pl.pallas_call(
        _proj_glu_kernel,
        grid=(s // tm, nb),
        in_specs=[pl.BlockSpec((tm, k), lambda i, j: (i, 0)),
                  pl.BlockSpec((k, tn), lambda i, j: (0, j)),
                  pl.BlockSpec((k, tn), lambda i, j: (0, j + nb)),
                  pl.BlockSpec((1, tn), lambda i, j: (0, j)),
                  pl.BlockSpec((1, tn), lambda i, j: (0, j + nb))],
        out_specs=pl.BlockSpec((tm, tn), lambda i, j: (i, j)),
        out_shape=jax.ShapeDtypeStruct((s, n), F32),
        compiler_params=_params(("arbitrary", "arbitrary"), 48),
        name=name,
    )(h, w, w, b2, b2)


def _out_proj_kernel(a_ref, w_ref, b_ref, x_ref, gp_ref, gn_ref, xo_ref, ho_ref):
    y = _dot(a_ref[...], w_ref[...]) + b_ref[...]
    xn, hn = _residual_norm(y, x_ref[...], gp_ref[...], gn_ref[...])
    xo_ref[...] = xn
    ho_ref[...] = hn


def _out_proj(a, w, b, x, g_post, g_next, tm=512, name="out_proj"):
    s, k = a.shape
    d = w.shape[1]
    row = pl.BlockSpec((1, d), lambda i: (0, 0))
    return pl.pallas_call(
        _out_proj_kernel,
        grid=(s // tm,),
        in_specs=[pl.BlockSpec((tm, k), lambda i: (i, 0)), pl.BlockSpec((k, d), lambda i: (0, 0)), row,
                  pl.BlockSpec((tm, d), lambda i: (i, 0)), row, row],
        out_specs=[pl.BlockSpec((tm, d), lambda i: (i, 0)), pl.BlockSpec((tm, d), lambda i: (i, 0))],
        out_shape=[jax.ShapeDtypeStruct((s, d), F32), jax.ShapeDtypeStruct((s, d), BF16)],
        compiler_params=_params(("arbitrary",), 56),
        name=name,
    )(a, w, _row(b), x, _row(g_post), _row(g_next))


def _t5_bucket(dist):
    n = np.maximum(dist, 0)
    large = REL_MAX_EXACT + (np.log(np.maximum(n, 1) / REL_MAX_EXACT)
                             / math.log(REL_MAX_DIST / REL_MAX_EXACT)
                             * (REL_BUCKETS - REL_MAX_EXACT)).astype(np.int32)
    large = np.minimum(large, REL_BUCKETS - 1)
    return np.where(n < REL_MAX_EXACT, n, large).astype(np.int32)


def _swa_geometry():
    qi = np.arange(A_BLOCK)[:, None]
    kj = np.arange(2 * A_BLOCK)[None, :]
    dist = qi + A_BLOCK - kj
    in_window = ((dist >= 0) & (dist < A_BLOCK)).astype(np.float32)
    return _t5_bucket(dist), np.concatenate([in_window, in_window], axis=1)


def _swa_bias_kernel(tab_ref, bucket_ref, o_ref):
    pair = pl.program_id(0)
    bucket = bucket_ref[...]
    for half in range(2):
        head = 2 * pair + half
        acc = jnp.zeros(bucket.shape, F32)
        for b in range(REL_BUCKETS):
            acc = jnp.where(bucket == b, tab_ref[b, head], acc)
        o_ref[0, :, half * 2 * A_BLOCK:(half + 1) * 2 * A_BLOCK] = acc


def _swa_bias(rel_table, bucket):
    return pl.pallas_call(
        _swa_bias_kernel,
        grid=(A_HEADS // 2,),
        in_specs=[pl.BlockSpec(memory_space=pltpu.SMEM),
                  pl.BlockSpec((A_BLOCK, 2 * A_BLOCK), lambda p: (0, 0))],
        out_specs=pl.BlockSpec((1, A_BLOCK, 4 * A_BLOCK), lambda p: (p, 0, 0)),
        out_shape=jax.ShapeDtypeStruct((A_HEADS // 2, A_BLOCK, 4 * A_BLOCK), F32),
        compiler_params=_params(("arbitrary",), 16),
        name="swa_bias",
    )(rel_table.astype(F32), jnp.asarray(bucket))


def _swa_kernel(sink_ref, q_ref, kvp_ref, kvc_ref, bias_ref, win_ref, o_ref):
    n = pl.program_id(0)
    hd = A_HEAD_DIM
    nk = 2 * A_BLOCK
    lane_k = lax.broadcasted_iota(jnp.int32, (nk, 2 * hd), 1)
    first_k = lane_k < hd
    col = lax.broadcasted_iota(jnp.int32, (A_BLOCK, 2 * nk), 1)
    key_valid = (n > 0) | ((col & (nk - 1)) >= A_BLOCK)
    mask = (win_ref[...] > 0.0) & key_valid
    lane_o = lax.broadcasted_iota(jnp.int32, (A_BLOCK, 2 * hd), 1)
    first_o = lane_o < hd
    pairs_per_kv = A_HEADS // A_KV_HEADS // 2
    kv_cols = A_KV_HEADS * 2 * hd
    for kvh in range(A_KV_HEADS):
        ks = slice(kvh * 2 * hd, (kvh + 1) * 2 * hd)
        vs = slice(kv_cols + kvh * 2 * hd, kv_cols + (kvh + 1) * 2 * hd)
        kk = jnp.concatenate([kvp_ref[:, ks], kvc_ref[:, ks]], axis=0)
        vv = jnp.concatenate([kvp_ref[:, vs], kvc_ref[:, vs]], axis=0)
        zero = jnp.zeros_like(kk)
        k_diag = jnp.concatenate([jnp.where(first_k, kk, zero), jnp.where(first_k, zero, kk)], axis=0)
        v_diag = jnp.concatenate([jnp.where(first_k, vv, zero), jnp.where(first_k, zero, vv)], axis=0)
        for j in range(pairs_per_kv):
            p = kvh * pairs_per_kv + j
            cols = slice(p * 2 * hd, (p + 1) * 2 * hd)
            q2 = q_ref[:, cols] * (hd ** -0.5)
            logits = _dot_nt(q2, k_diag)
            logits = jnp.where(mask, logits + bias_ref[p], NEG_INF)
            es, invs = [], []
            for half in range(2):
                lg = logits[:, half * nk:(half + 1) * nk]
                sink = sink_ref[2 * p + half]
                m = jnp.maximum(jnp.max(lg, axis=-1, keepdims=True), sink)
                e = jnp.exp(lg - m)
                den = jnp.sum(e, axis=-1, keepdims=True) + jnp.exp(sink - m)
                es.append(e.astype(BF16))
                invs.append(1.0 / den)
            o2 = _dot(jnp.concatenate(es, axis=1), v_diag)
            o_ref[:, cols] = (o2 * jnp.where(first_o, invs[0], invs[1])).astype(o_ref.dtype)


def _swa_attention(qkv, sinks, bias, window):
    s = qkv.shape[0]
    nq = A_HEADS * A_HEAD_DIM
    nkv = 2 * A_KV_HEADS * 2 * A_HEAD_DIM
    kv_blk = nq // nkv
    return pl.pallas_call(
        _swa_kernel,
        grid=(s // A_BLOCK,),
        in_specs=[pl.BlockSpec(memory_space=pltpu.SMEM),
                  pl.BlockSpec((A_BLOCK, nq), lambda n: (n, 0)),
                  pl.BlockSpec((A_BLOCK, nkv), lambda n: (jnp.maximum(n - 1, 0), kv_blk)),
                  pl.BlockSpec((A_BLOCK, nkv), lambda n: (n, kv_blk)),
                  pl.BlockSpec((A_HEADS // 2, A_BLOCK, 4 * A_BLOCK), lambda n: (0, 0, 0)),
                  pl.BlockSpec((A_BLOCK, 4 * A_BLOCK), lambda n: (0, 0))],
        out_specs=pl.BlockSpec((A_BLOCK, nq), lambda n: (n, 0)),
        out_shape=jax.ShapeDtypeStruct((s, nq), BF16),
        compiler_params=_params(("arbitrary",), 32),
        name="swa_attention",
    )(sinks.astype(F32), qkv, qkv, qkv, bias, window)


def _swa_layer(h, x, w_qkv, sinks, w_o, rel_table, g_post, g_next):
    nq = A_HEADS * A_HEAD_DIM
    nkv = A_KV_HEADS * A_HEAD_DIM
    wq = w_qkv[:, :nq]
    wk = w_qkv[:, nq:nq + nkv].reshape(D_MODEL, A_KV_HEADS, 1, A_HEAD_DIM)
    wv = w_qkv[:, nq + nkv:].reshape(D_MODEL, A_KV_HEADS, 1, A_HEAD_DIM)
    dup = lambda w: jnp.broadcast_to(w, (D_MODEL, A_KV_HEADS, 2, A_HEAD_DIM)).reshape(D_MODEL, 2 * nkv)
    w_ext = jnp.concatenate([wq, dup(wk), dup(wv)], axis=1).astype(BF16)
    qkv = _proj(h, w_ext, BF16, name="swa_qkv")
    bucket, window = _swa_geometry()
    bias = _swa_bias(rel_table, bucket)
    o = _swa_attention(qkv, sinks, bias, jnp.asarray(window))
    return _out_proj(o, w_o.astype(BF16), jnp.zeros((D_MODEL,), F32), x, g_post, g_next, name="swa_out")


def _gla_gate_kernel(h_ref, w1_ref, w2_ref, b_ref, o_ref):
    t = _dot(h_ref[...], w1_ref[...]).astype(BF16)
    gk = _dot(t, w2_ref[...]) + b_ref[...]
    log_sig = -(jnp.maximum(-gk, 0.0) + jnp.log1p(jnp.exp(-jnp.abs(gk))))
    o_ref[...] = log_sig / B_GATE_TAU


def _gla_gate(h, w1, w2, b, tm=512):
    s, d = h.shape
    n = w2.shape[1]
    w1p = jnp.zeros((d, LANES), BF16).at[:, :B_GATE_RANK].set(w1.astype(BF16))
    w2p = jnp.zeros((LANES, n), BF16).at[:B_GATE_RANK, :].set(w2.astype(BF16))
    return pl.pallas_call(
        _gla_gate_kernel,
        grid=(s // tm,),
        in_specs=[pl.BlockSpec((tm, d), lambda i: (i, 0)), pl.BlockSpec((d, LANES), lambda i: (0, 0)),
                  pl.BlockSpec((LANES, n), lambda i: (0, 0)), pl.BlockSpec((1, n), lambda i: (0, 0))],
        out_specs=pl.BlockSpec((tm, n), lambda i: (i, 0)),
        out_shape=jax.ShapeDtypeStruct((s, n), F32),
        compiler_params=_params(("arbitrary",), 32),
        name="gla_gate",
    )(h, w1p, w2p, _row(b))


def _gla_kernel(qkvr_ref, la_ref, onorm_ref, o_ref, st_ref, *, n_sub):
    @pl.when(pl.program_id(0) == 0)
    def _():
        st_ref[...] = jnp.zeros_like(st_ref)

    c = B_CHUNK
    dk_all = B_HEADS * B_KEY_DIM
    dv_all = B_HEADS * B_VAL_DIM
    causal = (lax.broadcasted_iota(jnp.int32, (c, c), 0) >= lax.broadcasted_iota(jnp.int32, (c, c), 1))
    tri = causal.astype(BF16)

    def chunk(ci, carry):
        rows = pl.ds(pl.multiple_of(ci * c, c), c)
        la = la_ref[rows, :]
        la_hi = la.astype(BF16)
        la_lo = (la - la_hi.astype(F32)).astype(BF16)
        cum = _dot(tri, la_hi) + _dot(tri, la_lo)
        last = cum[c - 1:c, :]
        q = qkvr_ref[rows, 0:dk_all].astype(F32) * (B_KEY_DIM ** -0.5)
        k = qkvr_ref[rows, dk_all:2 * dk_all].astype(F32)
        q_dec = (q * jnp.exp(cum)).astype(BF16)
        k_inv = (k * jnp.exp(-cum)).astype(BF16)
        k_end = (k * jnp.exp(last - cum)).astype(BF16)
        decay = jnp.exp(last)
        for hh in range(B_HEADS):
            ks = slice(hh * B_KEY_DIM, (hh + 1) * B_KEY_DIM)
            v = qkvr_ref[rows, 2 * dk_all + hh * B_VAL_DIM:2 * dk_all + (hh + 1) * B_VAL_DIM]
            r = qkvr_ref[rows, 2 * dk_all + dv_all + hh * B_VAL_DIM:
                         2 * dk_all + dv_all + (hh + 1) * B_VAL_DIM].astype(F32)
            att = jnp.where(causal, _dot_nt(q_dec[:, ks], k_inv[:, ks]), 0.0).astype(BF16)
            st = st_ref[hh]
            o = _dot(att, v) + _dot_nt(q_dec[:, ks], st.astype(BF16))
            st_ref[hh] = st * decay[:, ks] + _dot_tn(v, k_end[:, ks])
            o = _rms(o) * onorm_ref[...]
            o_ref[rows, hh * B_VAL_DIM:(hh + 1) * B_VAL_DIM] = (o * _silu(r)).astype(o_ref.dtype)
        return carry

    lax.fori_loop(0, n_sub, chunk, 0)


def _gla_core(qkvr, log_a, o_norm, tb=256):
    s = qkvr.shape[0]
    dv_all = B_HEADS * B_VAL_DIM
    return pl.pallas_call(
        functools.partial(_gla_kernel, n_sub=tb // B_CHUNK),
        grid=(s // tb,),
        in_specs=[pl.BlockSpec((tb, qkvr.shape[1]), lambda i: (i, 0)),
                  pl.BlockSpec((tb, log_a.shape[1]), lambda i: (i, 0)),
                  pl.BlockSpec((1, B_VAL_DIM), lambda i: (0, 0))],
        out_specs=pl.BlockSpec((tb, dv_all), lambda i: (i, 0)),
        out_shape=jax.ShapeDtypeStruct((s, dv_all), BF16),
        scratch_shapes=[pltpu.VMEM((B_HEADS, B_VAL_DIM, B_KEY_DIM), F32)],
        compiler_params=_params(("arbitrary",), 32),
        name="gla_core",
    )(qkvr, log_a, _row(o_norm))


def _gla_layer(h, x, w_qkvr, w_g1, w_g2, g_bias, o_norm, w_o, g_post, g_next):
    qkvr = _proj(h, w_qkvr.astype(BF16), BF16, name="gla_qkvr")
    log_a = _gla_gate(h, w_g1, w_g2, g_bias)
    o = _gla_core(qkvr, log_a, o_norm)
    return _out_proj(o, w_o.astype(BF16), jnp.zeros((D_MODEL,), F32), x, g_post, g_next, name="gla_out")


def _dwconv_ln_kernel(u_ref, uprev_ref, w_ref, b_ref, lng_ref, lnb_ref, o_ref, ext_ref, z_ref, *, tm, rc, cw):
    i = pl.program_id(0)
    ext_ref[0:C_HALO, :] = jnp.where(i > 0, uprev_ref[...], 0.0)
    ext_ref[C_HALO:C_HALO + tm, :] = u_ref[...]
    d = u_ref.shape[1]
    first = C_HALO - (C_KERNEL - 1)
    sub = 8
    for r0 in range(0, tm, rc):
        for c0 in range(0, d, cw):
            cols = slice(c0, c0 + cw)
            acc = jnp.zeros((rc, cw), F32) + b_ref[:, cols]
            for b in range(sub):
                rows = rc if b == 0 else rc + sub
                part = None
                for s in range(first, first + C_KERNEL):
                    if s % sub != b:
                        continue
                    term = w_ref[s - first:s - first + 1, cols] * ext_ref[r0 + s - b:r0 + s - b + rows, cols]
                    part = term if part is None else part + term
                if b:
                    part = pltpu.roll(part, rows - b, axis=0)[0:rc, :]
                acc = acc + part
            z_ref[r0:r0 + rc, cols] = acc
    z = z_ref[...]
    zc = z - jnp.mean(z, axis=-1, keepdims=True)
    zn = zc * lax.rsqrt(jnp.mean(zc * zc, axis=-1, keepdims=True) + EPS) * lng_ref[...] + lnb_ref[...]
    o_ref[...] = _silu(zn).astype(o_ref.dtype)


def _dwconv_ln(u, w_dw, b_dw, ln_g, ln_b, tm=128, rc=64, cw=512):
    s, d = u.shape
    row = pl.BlockSpec((1, d), lambda i: (0, 0))
    halo_blocks = tm // C_HALO
    return pl.pallas_call(
        functools.partial(_dwconv_ln_kernel, tm=tm, rc=rc, cw=cw),
        grid=(s // tm,),
        in_specs=[pl.BlockSpec((tm, d), lambda i: (i, 0)),
                  pl.BlockSpec((C_HALO, d), lambda i: (jnp.maximum(i * halo_blocks - 1, 0), 0)),
                  pl.BlockSpec((C_KERNEL, d), lambda i: (0, 0)), row, row, row],
        out_specs=pl.BlockSpec((tm, d), lambda i: (i, 0)),
        out_shape=jax.ShapeDtypeStruct((s, d), BF16),
        scratch_shapes=[pltpu.VMEM((C_HALO + tm, d), F32), pltpu.VMEM((tm, d), F32)],
        compiler_params=_params(("arbitrary",), 32),
        name="dwconv_ln",
    )(u, u, w_dw.astype(F32), _row(b_dw), _row(ln_g), _row(ln_b))


def _conformer_layer(h, x, w_pw1, b_pw1, w_dw, b_dw, ln_g, ln_b, w_pw2, b_pw2, g_post, g_next):
    u = _proj_glu(h, w_pw1.astype(BF16), b_pw1, name="conf_pw1")
    z = _dwconv_ln(u, w_dw, b_dw, ln_g, ln_b)
    return _out_proj(z, w_pw2.astype(BF16), b_pw2, x, g_post, g_next, name="conf_out")


def _sgu_kernel(u_ref, v_ref, ws_ref, bs_ref, lng_ref, lnb_ref, wout_ref, x_ref, gp_ref, gn_ref,
                xo_ref, ho_ref, p_ref, *, tm):
    v = v_ref[...]
    vc = v - jnp.mean(v, axis=-1, keepdims=True)
    vn = (vc * lax.rsqrt(jnp.mean(vc * vc, axis=-1, keepdims=True) + EPS) * lng_ref[...]
          + lnb_ref[...]).astype(BF16)
    t = D_CHUNK
    gw = D_HALF // D_GROUPS
    tril = lax.broadcasted_iota(jnp.int32, (t, t), 0) >= lax.broadcasted_iota(jnp.int32, (t, t), 1)
    for g in range(D_GROUPS):
        w = jnp.where(tril, ws_ref[g], jnp.zeros((t, t), BF16))
        cols = slice(g * gw, (g + 1) * gw)
        for c0 in range(0, tm, t):
            sv = _dot(w, vn[c0:c0 + t, cols]) + bs_ref[g]
            p_ref[c0:c0 + t, cols] = (u_ref[c0:c0 + t, cols].astype(F32) * sv).astype(BF16)
    y = _dot(p_ref[...], wout_ref[...])
    xn, hn = _residual_norm(y, x_ref[...], gp_ref[...], gn_ref[...])
    xo_ref[...] = xn
    ho_ref[...] = hn


def _sgu(u, v, w_s, b_s, ln_g, ln_b, w_out, x, g_post, g_next, tm=256):
    s, dh = u.shape
    d = w_out.shape[1]
    row = pl.BlockSpec((1, d), lambda i: (0, 0))
    rowh = pl.BlockSpec((1, dh), lambda i: (0, 0))
    return pl.pallas_call(
        functools.partial(_sgu_kernel, tm=tm),
        grid=(s // tm,),
        in_specs=[pl.BlockSpec((tm, dh), lambda i: (i, 0)), pl.BlockSpec((tm, dh), lambda i: (i, 0)),
                  pl.BlockSpec((D_GROUPS, D_CHUNK, D_CHUNK), lambda i: (0, 0, 0)),
                  pl.BlockSpec((D_GROUPS, D_CHUNK, 1), lambda i: (0, 0, 0)),
                  rowh, rowh,
                  pl.BlockSpec((dh, d), lambda i: (0, 0), pipeline_mode=pl.Buffered(1)),
                  pl.BlockSpec((tm, d), lambda i: (i, 0)), row, row],
        out_specs=[pl.BlockSpec((tm, d), lambda i: (i, 0)), pl.BlockSpec((tm, d), lambda i: (i, 0))],
        out_shape=[jax.ShapeDtypeStruct((s, d), F32), jax.ShapeDtypeStruct((s, d), BF16)],
        scratch_shapes=[pltpu.VMEM((tm, dh), BF16)],
        compiler_params=_params(("arbitrary",), 56),
        name="sgu",
    )(u, v, w_s.astype(BF16), b_s.astype(F32)[:, :, None], _row(ln_g), _row(ln_b), w_out.astype(BF16),
      x, _row(g_post), _row(g_next))


def _gmlp_layer(h, x, w_in, b_in, ln_g, ln_b, w_s, b_s, w_out, g_post, g_next):
    w_in = w_in.astype(BF16)
    u = _proj_gelu(h, w_in[:, :D_HALF], b_in[:D_HALF], BF16, name="gmlp_in_u")
    v = _proj_gelu(h, w_in[:, D_HALF:], b_in[D_HALF:], F32, name="gmlp_in_v")
    return _sgu(u, v, w_s, b_s, ln_g, ln_b, w_out, x, g_post, g_next)


def _mem_kv_kernel(mem_ref, g_ref, w_ref, o_ref):
    mem_n = (_rms(mem_ref[...]) * g_ref[...]).astype(BF16)
    o_ref[...] = _dot(mem_n, w_ref[...]).astype(o_ref.dtype)


def _mem_kv(mem, g, w_kv):
    m, d = mem.shape
    n = w_kv.shape[1]
    tn = n // 2
    return pl.pallas_call(
        _mem_kv_kernel,
        grid=(n // tn,),
        in_specs=[pl.BlockSpec((m, d), lambda j: (0, 0)), pl.BlockSpec((1, d), lambda j: (0, 0)),
                  pl.BlockSpec((d, tn), lambda j: (0, j))],
        out_specs=pl.BlockSpec((m, tn), lambda j: (0, j)),
        out_shape=jax.ShapeDtypeStruct((m, n), BF16),
        compiler_params=_params(("arbitrary",), 32),
        name="mem_kv",
    )(mem, _row(g), w_kv.astype(BF16))


def _xattn_kernel(h_ref, wq_ref, kv_ref, wo_ref, x_ref, gp_ref, gn_ref, xo_ref, ho_ref):
    hd = X_HEAD_DIM
    nq = X_HEADS * hd
    q = (_dot(h_ref[...], wq_ref[...]) * (hd ** -0.5)).astype(BF16)
    outs = []
    for hh in range(X_HEADS):
        cols = slice(hh * hd, (hh + 1) * hd)
        logits = _dot_nt(q[:, cols], kv_ref[:, cols])
        e = jnp.exp(logits - jnp.max(logits, axis=-1, keepdims=True))
        inv = 1.0 / jnp.sum(e, axis=-1, keepdims=True)
        o = _dot(e.astype(BF16), kv_ref[:, nq + hh * hd:nq + (hh + 1) * hd])
        outs.append((o * inv).astype(BF16))
    y = _dot(jnp.concatenate(outs, axis=1), wo_ref[...])
    xn, hn = _residual_norm(y, x_ref[...], gp_ref[...], gn_ref[...])
    xo_ref[...] = xn
    ho_ref[...] = hn


def _xattn(h, x, w_q, kv, w_o, g_post, g_next, tm=512):
    s, d = h.shape
    nq = w_q.shape[1]
    row = pl.BlockSpec((1, d), lambda i: (0, 0))
    return pl.pallas_call(
        _xattn_kernel,
        grid=(s // tm,),
        in_specs=[pl.BlockSpec((tm, d), lambda i: (i, 0)), pl.BlockSpec((d, nq), lambda i: (0, 0)),
                  pl.BlockSpec(kv.shape, lambda i: (0, 0)), pl.BlockSpec((nq, d), lambda i: (0, 0)),
                  pl.BlockSpec((tm, d), lambda i: (i, 0)), row, row],
        out_specs=[pl.BlockSpec((tm, d), lambda i: (i, 0)), pl.BlockSpec((tm, d), lambda i: (i, 0))],
        out_shape=[jax.ShapeDtypeStruct((s, d), F32), jax.ShapeDtypeStruct((s, d), BF16)],
        compiler_params=_params(("arbitrary",), 48),
        name="xattn",
    )(h, w_q.astype(BF16), kv, w_o.astype(BF16), x, _row(g_post), _row(g_next))


def _ffn_kernel(h_ref, wg_ref, wu_ref, wc_ref, bc_ref, wd_ref, x_ref, gp_ref, *rest, tm, emit_h):
    if emit_h:
        gn_ref, xo_ref, ho_ref, acc_ref, halo_ref, ext_ref = rest
    else:
        xo_ref, acc_ref, halo_ref, ext_ref = rest
    i = pl.program_id(0)
    f = pl.program_id(1)

    @pl.when(i == 0)
    def _():
        halo_ref[f] = jnp.zeros(halo_ref.shape[1:], F32)

    @pl.when(f == 0)
    def _():
        acc_ref[...] = jnp.zeros_like(acc_ref)

    h = h_ref[...]
    gate = _dot(h, wg_ref[...])
    up = _dot(h, wu_ref[...])
    ext_ref[0:FFN_HALO, :] = halo_ref[f]
    ext_ref[FFN_HALO:FFN_HALO + tm, :] = gate
    halo_ref[f] = gate[tm - FFN_HALO:tm, :]
    conv = (wc_ref[2:3, :] * gate
            + wc_ref[1:2, :] * ext_ref[FFN_HALO - 1:FFN_HALO - 1 + tm, :]
            + wc_ref[0:1, :] * ext_ref[FFN_HALO - 2:FFN_HALO - 2 + tm, :]
            + bc_ref[...])
    act = (_gelu_tanh(conv) * up).astype(BF16)
    acc_ref[...] += _dot(act, wd_ref[...])

    @pl.when(f == pl.num_programs(1) - 1)
    def _():
        xn, hn = _residual_norm(acc_ref[...], x_ref[...], gp_ref[...], gn_ref[...] if emit_h else None)
        xo_ref[...] = xn
        if emit_h:
            ho_ref[...] = hn


def _ffn(h, x, w_gate_up, w_conv, b_conv, w_down, g_post, g_next, tm=512, fc=512):
    s, d = h.shape
    ff = w_down.shape[0]
    nf = ff // fc
    emit_h = g_next is not None
    row = pl.BlockSpec((1, d), lambda i, f: (0, 0))
    tile = pl.BlockSpec((tm, d), lambda i, f: (i, 0))
    in_specs = [tile,
                pl.BlockSpec((d, fc), lambda i, f: (0, f)),
                pl.BlockSpec((d, fc), lambda i, f: (0, f + nf)),
                pl.BlockSpec((FFN_KERNEL, fc), lambda i, f: (0, f)),
                pl.BlockSpec((1, fc), lambda i, f: (0, f)),
                pl.BlockSpec((fc, d), lambda i, f: (f, 0)),
                tile, row]
    args = [h, w_gate_up, w_gate_up, w_conv.astype(F32), _row(b_conv), w_down, x, _row(g_post)]
    out_specs = [tile]
    out_shape = [jax.ShapeDtypeStruct((s, d), F32)]
    if emit_h:
        in_specs.append(row)
        args.append(_row(g_next))
        out_specs.append(tile)
        out_shape.append(jax.ShapeDtypeStruct((s, d), BF16))
    outs = pl.pallas_call(
        functools.partial(_ffn_kernel, tm=tm, emit_h=emit_h),
        grid=(s // tm, nf),
        in_specs=in_specs,
        out_specs=out_specs,
        out_shape=out_shape,
        scratch_shapes=[pltpu.VMEM((tm, d), F32), pltpu.VMEM((nf, FFN_HALO, fc), F32),
                        pltpu.VMEM((FFN_HALO + tm, fc), F32)],
        compiler_params=_params(("arbitrary", "arbitrary"), 56),
        name="ffn",
    )(*args)
    return (outs[0], outs[1]) if emit_h else (outs[0], None)


def kernel(x, mem, norm_mix_pre, norm_mix_post, norm_mem, norm_xattn_pre, norm_xattn_post, norm_ffn_pre, norm_ffn_post, rel_bias_table, a_w_qkv, a_sinks, a_w_o, b_w_qkvr, b_w_gate1, b_w_gate2, b_gate_bias, b_o_norm, b_w_o, c_w_pw1, c_b_pw1, c_w_dw, c_b_dw, c_ln_g, c_ln_b, c_w_pw2, c_b_pw2, d_w_in, d_b_in, d_ln_g, d_ln_b, d_w_s, d_b_s, d_w_out, x_w_q, x_w_kv, x_w_o, f_w_gate_up, f_w_conv, f_b_conv, f_w_down):
    assert x.shape[0] == 1 and mem.shape[0] == 1
    xs = x[0]
    mem2 = mem[0]
    h = _rmsnorm(xs, norm_mix_pre[0])
    for i in range(DEPTH):
        kind, j = i % 4, i // 4
        g_post, g_next = norm_mix_post[i], norm_xattn_pre[i]
        if kind == 0:
            xs, h = _swa_layer(h, xs, a_w_qkv[j], a_sinks[j], a_w_o[j], rel_bias_table, g_post, g_next)
        elif kind == 1:
            xs, h = _gla_layer(h, xs, b_w_qkvr[j], b_w_gate1[j], b_w_gate2[j], b_gate_bias[j], b_o_norm[j],
                               b_w_o[j], g_post, g_next)
        elif kind == 2:
            xs, h = _conformer_layer(h, xs, c_w_pw1[j], c_b_pw1[j], c_w_dw[j], c_b_dw[j], c_ln_g[j],
                                     c_ln_b[j], c_w_pw2[j], c_b_pw2[j], g_post, g_next)
        else:
            xs, h = _gmlp_layer(h, xs, d_w_in[j], d_b_in[j], d_ln_g[j], d_ln_b[j], d_w_s[j], d_b_s[j],
                                d_w_out[j], g_post, g_next)
        kv = _mem_kv(mem2, norm_mem[i], x_w_kv[i])
        xs, h = _xattn(h, xs, x_w_q[i], kv, x_w_o[i], norm_xattn_post[i], norm_ffn_pre[i])
        g_next = norm_mix_pre[i + 1] if i + 1 < DEPTH else None
        xs, h = _ffn(h, xs, f_w_gate_up[i].astype(BF16), f_w_conv[i], f_b_conv[i], f_w_down[i].astype(BF16),
                     norm_ffn_post[i], g_next)
    return xs[None]
```

```python
import functools
import math

import numpy as np
import jax
import jax.numpy as jnp
from jax import lax
from jax.experimental import pallas as pl
from jax.experimental.pallas import tpu as pltpu

D_MODEL = 2048
DEPTH = 4
EPS = 1e-6
NEG_INF = -1e30

A_HEADS = 32
A_KV_HEADS = 4
A_HEAD_DIM = 64
A_BLOCK = 128
REL_BUCKETS = 32
REL_MAX_EXACT = 16
REL_MAX_DIST = 128

B_HEADS = 4
B_KEY_DIM = 256
B_VAL_DIM = 512
B_GATE_RANK = 16
B_GATE_TAU = 16.0
B_CHUNK = 64

C_KERNEL = 31
C_HALO = 32

D_CHUNK = 128
D_GROUPS = 8
D_HALF = 2 * D_MODEL

X_HEADS = 4
X_HEAD_DIM = 128

FFN_DIM = 4 * D_MODEL
FFN_KERNEL = 3
FFN_HALO = 8

LANES = 128
MIB = 1024 * 1024

BF16 = jnp.bfloat16
F32 = jnp.float32


def _params(semantics, vmem_mib):
    return pltpu.CompilerParams(dimension_semantics=semantics, vmem_limit_bytes=vmem_mib * MIB)


def _dot(a, b):
    return jnp.dot(a, b, preferred_element_type=F32)


def _dot_nt(a, b):
    return lax.dot_general(a, b, (((1,), (1,)), ((), ())), preferred_element_type=F32)


def _dot_tn(a, b):
    return lax.dot_general(a, b, (((0,), (0,)), ((), ())), preferred_element_type=F32)


def _rms(y):
    return y * lax.rsqrt(jnp.mean(y * y, axis=-1, keepdims=True) + EPS)


def _residual_norm(y, x, g_post, g_next):
    xn = x + _rms(y) * g_post
    if g_next is None:
        return xn, None
    return xn, (_rms(xn) * g_next).astype(BF16)


def _gelu_tanh(x):
    return 0.5 * x * (1.0 + jnp.tanh(math.sqrt(2.0 / math.pi) * (x + 0.044715 * (x * x * x))))


def _gelu_erf(x):
    return 0.5 * x * (1.0 + lax.erf(x * math.sqrt(0.5)))


def _silu(x):
    return x * jax.nn.sigmoid(x)


def _row(v):
    return v.reshape(1, -1).astype(F32)


def _rmsnorm_kernel(x_ref, g_ref, o_ref):
    o_ref[...] = (_rms(x_ref[...]) * g_ref[...]).astype(o_ref.dtype)


def _rmsnorm(x, g, tm=512):
    s, d = x.shape
    return pl.pallas_call(
        _rmsnorm_kernel,
        grid=(s // tm,),
        in_specs=[pl.BlockSpec((tm, d), lambda i: (i, 0)), pl.BlockSpec((1, d), lambda i: (0, 0))],
        out_specs=pl.BlockSpec((tm, d), lambda i: (i, 0)),
        out_shape=jax.ShapeDtypeStruct((s, d), BF16),
        compiler_params=_params(("arbitrary",), 32),
        name="rmsnorm",
    )(x, _row(g))


def _proj_plain_kernel(h_ref, w_ref, o_ref):
    o_ref[...] = _dot(h_ref[...], w_ref[...]).astype(o_ref.dtype)


def _proj_gelu_kernel(h_ref, w_ref, b_ref, o_ref):
    o_ref[...] = _gelu_erf(_dot(h_ref[...], w_ref[...]) + b_ref[...]).astype(o_ref.dtype)


def _proj_glu_kernel(h_ref, wa_ref, wg_ref, ba_ref, bg_ref, o_ref):
    h = h_ref[...]
    a = _dot(h, wa_ref[...]) + ba_ref[...]
    g = _dot(h, wg_ref[...]) + bg_ref[...]
    o_ref[...] = (a * jax.nn.sigmoid(g)).astype(o_ref.dtype)


def _proj(h, w, out_dtype, tm=1024, tn=1024, name="proj"):
    s, k = h.shape
    n = w.shape[1]
    tn = min(tn, n)
    return pl.pallas_call(
        _proj_plain_kernel,
        grid=(s // tm, n // tn),
        in_specs=[pl.BlockSpec((tm, k), lambda i, j: (i, 0)), pl.BlockSpec((k, tn), lambda i, j: (0, j))],
        out_specs=pl.BlockSpec((tm, tn), lambda i, j: (i, j)),
        out_shape=jax.ShapeDtypeStruct((s, n), out_dtype),
        compiler_params=_params(("arbitrary", "arbitrary"), 48),
        name=name,
    )(h, w)


def _proj_gelu(h, w, b, out_dtype, tm=1024, tn=1024, name="proj_gelu"):
    s, k = h.shape
    n = w.shape[1]
    return pl.pallas_call(
        _proj_gelu_kernel,
        grid=(s // tm, n // tn),
        in_specs=[pl.BlockSpec((tm, k), lambda i, j: (i, 0)), pl.BlockSpec((k, tn), lambda i, j: (0, j)),
                  pl.BlockSpec((1, tn), lambda i, j: (0, j))],
        out_specs=pl.BlockSpec((tm, tn), lambda i, j: (i, j)),
        out_shape=jax.ShapeDtypeStruct((s, n), out_dtype),
        compiler_params=_params(("arbitrary", "arbitrary"), 48),
        name=name,
    )(h, w, _row(b))


def _proj_glu(h, w, b, tm=1024, tn=512, name="proj_glu"):
    s, k = h.shape
    n = w.shape[1] // 2
    nb = n // tn
    b2 = _row(b)
    return pl.pallas_call(
        _proj_glu_kernel,
        grid=(s // tm, nb),
        in_specs=[pl.BlockSpec((tm, k), lambda i, j: (i, 0)),
                  pl.BlockSpec((k, tn), lambda i, j: (0, j)),
                  pl.BlockSpec((k, tn), lambda i, j: (0, j + nb)),
                  pl.BlockSpec((1, tn), lambda i, j: (0, j)),
                  pl.BlockSpec((1, tn), lambda i, j: (0, j + nb))],
        out_specs=pl.BlockSpec((tm, tn), lambda i, j: (i, j)),
        out_shape=jax.ShapeDtypeStruct((s, n), F32),
        compiler_params=_params(("arbitrary", "arbitrary"), 48),
        name=name,
    )(h, w, w, b2, b2)


def _out_proj_kernel(a_ref, w_ref, b_ref, x_ref, gp_ref, gn_ref, xo_ref, ho_ref, *, n_split):
    rows = a_ref.shape[0] // n_split
    for r0 in range(0, a_ref.shape[0], rows):
        rs = slice(r0, r0 + rows)
        y = _dot(a_ref[rs, :], w_ref[...]) + b_ref[...]
        xn, hn = _residual_norm(y, x_ref[rs, :], gp_ref[...], gn_ref[...])
        xo_ref[rs, :] = xn
        ho_ref[rs, :] = hn


def _out_proj(a, w, b, x, g_post, g_next, tm=512, name="out_proj"):
    s, k = a.shape
    d = w.shape[1]
    row = pl.BlockSpec((1, d), lambda i: (0, 0))
    return pl.pallas_call(
        functools.partial(_out_proj_kernel, n_split=4),
        grid=(s // tm,),
        in_specs=[pl.BlockSpec((tm, k), lambda i: (i, 0)), pl.BlockSpec((k, d), lambda i: (0, 0)), row,
                  pl.BlockSpec((tm, d), lambda i: (i, 0)), row, row],
        out_specs=[pl.BlockSpec((tm, d), lambda i: (i, 0)), pl.BlockSpec((tm, d), lambda i: (i, 0))],
        out_shape=[jax.ShapeDtypeStruct((s, d), F32), jax.ShapeDtypeStruct((s, d), BF16)],
        compiler_params=_params(("arbitrary",), 56),
        name=name,
    )(a, w, _row(b), x, _row(g_post), _row(g_next))


def _t5_bucket(dist):
    n = np.maximum(dist, 0)
    large = REL_MAX_EXACT + (np.log(np.maximum(n, 1) / REL_MAX_EXACT)
                             / math.log(REL_MAX_DIST / REL_MAX_EXACT)
                             * (REL_BUCKETS - REL_MAX_EXACT)).astype(np.int32)
    large = np.minimum(large, REL_BUCKETS - 1)
    return np.where(n < REL_MAX_EXACT, n, large).astype(np.int32)


def _swa_geometry():
    qi = np.arange(A_BLOCK)[:, None]
    kj = np.arange(2 * A_BLOCK)[None, :]
    dist = qi + A_BLOCK - kj
    in_window = ((dist >= 0) & (dist < A_BLOCK)).astype(np.float32)
    return _t5_bucket(dist), np.tile(in_window, (A_HEADS // A_KV_HEADS // 2, 2))


def _swa_bias_kernel(tab_ref, bucket_ref, o_ref):
    pair = pl.program_id(0)
    bucket = bucket_ref[...]
    for half in range(2):
        head = 2 * pair + half
        acc = jnp.zeros(bucket.shape, F32)
        for b in range(REL_BUCKETS):
            acc = jnp.where(bucket == b, tab_ref[b, head], acc)
        o_ref[0, :, half * 2 * A_BLOCK:(half + 1) * 2 * A_BLOCK] = acc


def _swa_bias(rel_table, bucket):
    return pl.pallas_call(
        _swa_bias_kernel,
        grid=(A_HEADS // 2,),
        in_specs=[pl.BlockSpec(memory_space=pltpu.SMEM),
                  pl.BlockSpec((A_BLOCK, 2 * A_BLOCK), lambda p: (0, 0))],
        out_specs=pl.BlockSpec((1, A_BLOCK, 4 * A_BLOCK), lambda p: (p, 0, 0)),
        out_shape=jax.ShapeDtypeStruct((A_HEADS // 2, A_BLOCK, 4 * A_BLOCK), F32),
        compiler_params=_params(("arbitrary",), 16),
        name="swa_bias",
    )(rel_table.astype(F32), jnp.asarray(bucket))


def _swa_kernel(sink_ref, q_ref, kvp_ref, kvc_ref, bias_ref, win_ref, o_ref):
    n = pl.program_id(0)
    hd = A_HEAD_DIM
    nk = 2 * A_BLOCK
    npair = A_HEADS // A_KV_HEADS // 2
    rows = npair * A_BLOCK
    first_k = lax.broadcasted_iota(jnp.int32, (nk, 2 * hd), 1) < hd
    col = lax.broadcasted_iota(jnp.int32, (rows, 2 * nk), 1)
    key_valid = (n > 0) | ((col & (nk - 1)) >= A_BLOCK)
    mask = (win_ref[...] > 0.0) & key_valid
    first_o = lax.broadcasted_iota(jnp.int32, (rows, 2 * hd), 1) < hd
    row_pair = lax.broadcasted_iota(jnp.int32, (rows, 1), 0) // A_BLOCK
    kv_cols = A_KV_HEADS * 2 * hd
    for kvh in range(A_KV_HEADS):
        ks = slice(kvh * 2 * hd, (kvh + 1) * 2 * hd)
        vs = slice(kv_cols + kvh * 2 * hd, kv_cols + (kvh + 1) * 2 * hd)
        kk = jnp.concatenate([kvp_ref[:, ks], kvc_ref[:, ks]], axis=0)
        vv = jnp.concatenate([kvp_ref[:, vs], kvc_ref[:, vs]], axis=0)
        zero = jnp.zeros_like(kk)
        k_diag = jnp.concatenate([jnp.where(first_k, kk, zero), jnp.where(first_k, zero, kk)], axis=0)
        v_diag = jnp.concatenate([jnp.where(first_k, vv, zero), jnp.where(first_k, zero, vv)], axis=0)
        p0 = kvh * npair
        pair_cols = [slice((p0 + j) * 2 * hd, (p0 + j + 1) * 2 * hd) for j in range(npair)]
        q4 = jnp.concatenate([q_ref[:, c] for c in pair_cols], axis=0) * (hd ** -0.5)
        logits = _dot_nt(q4, k_diag)
        bias = bias_ref[p0:p0 + npair].reshape(rows, 2 * nk)
        logits = jnp.where(mask, logits + bias, NEG_INF)
        es, invs = [], []
        for half in range(2):
            lg = logits[:, half * nk:(half + 1) * nk]
            sink = jnp.full((rows, 1), sink_ref[2 * p0 + half], F32)
            for j in range(1, npair):
                sink = jnp.where(row_pair >= j, sink_ref[2 * (p0 + j) + half], sink)
            m = jnp.maximum(jnp.max(lg, axis=-1, keepdims=True), sink)
            e = jnp.exp(lg - m)
            den = jnp.sum(e, axis=-1, keepdims=True) + jnp.exp(sink - m)
            es.append(e.astype(BF16))
            invs.append(1.0 / den)
        o4 = _dot(jnp.concatenate(es, axis=1), v_diag)
        o4 = (o4 * jnp.where(first_o, invs[0], invs[1])).astype(o_ref.dtype)
        for j in range(npair):
            o_ref[:, pair_cols[j]] = o4[j * A_BLOCK:(j + 1) * A_BLOCK, :]


def _swa_attention(qkv, sinks, bias, window):
    s = qkv.shape[0]
    nq = A_HEADS * A_HEAD_DIM
    nkv = 2 * A_KV_HEADS * 2 * A_HEAD_DIM
    kv_blk = nq // nkv
    return pl.pallas_call(
        _swa_kernel,
        grid=(s // A_BLOCK,),
        in_specs=[pl.BlockSpec(memory_space=pltpu.SMEM),
                  pl.BlockSpec((A_BLOCK, nq), lambda n: (n, 0)),
                  pl.BlockSpec((A_BLOCK, nkv), lambda n: (jnp.maximum(n - 1, 0), kv_blk)),
                  pl.BlockSpec((A_BLOCK, nkv), lambda n: (n, kv_blk)),
                  pl.BlockSpec((A_HEADS // 2, A_BLOCK, 4 * A_BLOCK), lambda n: (0, 0, 0)),
                  pl.BlockSpec(window.shape, lambda n: (0, 0))],
        out_specs=pl.BlockSpec((A_BLOCK, nq), lambda n: (n, 0)),
        out_shape=jax.ShapeDtypeStruct((s, nq), BF16),
        compiler_params=_params(("arbitrary",), 32),
        name="swa_attention",
    )(sinks.astype(F32), qkv, qkv, qkv, bias, window)


def _swa_layer(h, x, w_qkv, sinks, w_o, rel_table, g_post, g_next):
    nq = A_HEADS * A_HEAD_DIM
    nkv = A_KV_HEADS * A_HEAD_DIM
    wq = w_qkv[:, :nq]
    wk = w_qkv[:, nq:nq + nkv].reshape(D_MODEL, A_KV_HEADS, 1, A_HEAD_DIM)
    wv = w_qkv[:, nq + nkv:].reshape(D_MODEL, A_KV_HEADS, 1, A_HEAD_DIM)
    dup = lambda w: jnp.broadcast_to(w, (D_MODEL, A_KV_HEADS, 2, A_HEAD_DIM)).reshape(D_MODEL, 2 * nkv)
    w_ext = jnp.concatenate([wq, dup(wk), dup(wv)], axis=1).astype(BF16)
    qkv = _proj(h, w_ext, BF16, name="swa_qkv")
    bucket, window = _swa_geometry()
    bias = _swa_bias(rel_table, bucket)
    o = _swa_attention(qkv, sinks, bias, jnp.asarray(window))
    return _out_proj(o, w_o.astype(BF16), jnp.zeros((D_MODEL,), F32), x, g_post, g_next, name="swa_out")


def _gla_gate_kernel(h_ref, w1_ref, w2_ref, b_ref, o_ref):
    t = _dot(h_ref[...], w1_ref[...]).astype(BF16)
    gk = _dot(t, w2_ref[...]) + b_ref[...]
    log_sig = -(jnp.maximum(-gk, 0.0) + jnp.log1p(jnp.exp(-jnp.abs(gk))))
    o_ref[...] = log_sig / B_GATE_TAU


def _gla_gate(h, w1, w2, b, tm=512):
    s, d = h.shape
    n = w2.shape[1]
    w1p = jnp.zeros((d, LANES), BF16).at[:, :B_GATE_RANK].set(w1.astype(BF16))
    w2p = jnp.zeros((LANES, n), BF16).at[:B_GATE_RANK, :].set(w2.astype(BF16))
    return pl.pallas_call(
        _gla_gate_kernel,
        grid=(s // tm,),
        in_specs=[pl.BlockSpec((tm, d), lambda i: (i, 0)), pl.BlockSpec((d, LANES), lambda i: (0, 0)),
                  pl.BlockSpec((LANES, n), lambda i: (0, 0)), pl.BlockSpec((1, n), lambda i: (0, 0))],
        out_specs=pl.BlockSpec((tm, n), lambda i: (i, 0)),
        out_shape=jax.ShapeDtypeStruct((s, n), F32),
        compiler_params=_params(("arbitrary",), 32),
        name="gla_gate",
    )(h, w1p, w2p, _row(b))


def _gla_kernel(qkvr_ref, la_ref, onorm_ref, o_ref, st_ref, *, n_sub):
    @pl.when(pl.program_id(0) == 0)
    def _():
        st_ref[...] = jnp.zeros_like(st_ref)

    c = B_CHUNK
    dk_all = B_HEADS * B_KEY_DIM
    dv_all = B_HEADS * B_VAL_DIM
    causal = (lax.broadcasted_iota(jnp.int32, (c, c), 0) >= lax.broadcasted_iota(jnp.int32, (c, c), 1))
    tri = causal.astype(BF16)

    def chunk(ci, carry):
        rows = pl.ds(pl.multiple_of(ci * c, c), c)
        la = la_ref[rows, :]
        la_hi = la.astype(BF16)
        la_lo = (la - la_hi.astype(F32)).astype(BF16)
        cum = _dot(tri, la_hi) + _dot(tri, la_lo)
        last = cum[c - 1:c, :]
        q = qkvr_ref[rows, 0:dk_all].astype(F32) * (B_KEY_DIM ** -0.5)
        k = qkvr_ref[rows, dk_all:2 * dk_all].astype(F32)
        q_dec = (q * jnp.exp(cum)).astype(BF16)
        k_inv = (k * jnp.exp(-cum)).astype(BF16)
        k_end = (k * jnp.exp(last - cum)).astype(BF16)
        decay = jnp.exp(last)
        for hh in range(B_HEADS):
            ks = slice(hh * B_KEY_DIM, (hh + 1) * B_KEY_DIM)
            v = qkvr_ref[rows, 2 * dk_all + hh * B_VAL_DIM:2 * dk_all + (hh + 1) * B_VAL_DIM]
            r = qkvr_ref[rows, 2 * dk_all + dv_all + hh * B_VAL_DIM:
                         2 * dk_all + dv_all + (hh + 1) * B_VAL_DIM].astype(F32)
            att = jnp.where(causal, _dot_nt(q_dec[:, ks], k_inv[:, ks]), 0.0).astype(BF16)
            st = st_ref[hh]
            o = _dot(att, v) + _dot_nt(q_dec[:, ks], st.astype(BF16))
            st_ref[hh] = st * decay[:, ks] + _dot_tn(v, k_end[:, ks])
            o = _rms(o) * onorm_ref[...]
            o_ref[rows, hh * B_VAL_DIM:(hh + 1) * B_VAL_DIM] = (o * _silu(r)).astype(o_ref.dtype)
        return carry

    lax.fori_loop(0, n_sub, chunk, 0)


def _gla_core(qkvr, log_a, o_norm, tb=256):
    s = qkvr.shape[0]
    dv_all = B_HEADS * B_VAL_DIM
    return pl.pallas_call(
        functools.partial(_gla_kernel, n_sub=tb // B_CHUNK),
        grid=(s // tb,),
        in_specs=[pl.BlockSpec((tb, qkvr.shape[1]), lambda i: (i, 0)),
                  pl.BlockSpec((tb, log_a.shape[1]), lambda i: (i, 0)),
                  pl.BlockSpec((1, B_VAL_DIM), lambda i: (0, 0))],
        out_specs=pl.BlockSpec((tb, dv_all), lambda i: (i, 0)),
        out_shape=jax.ShapeDtypeStruct((s, dv_all), BF16),
        scratch_shapes=[pltpu.VMEM((B_HEADS, B_VAL_DIM, B_KEY_DIM), F32)],
        compiler_params=_params(("arbitrary",), 32),
        name="gla_core",
    )(qkvr, log_a, _row(o_norm))


def _gla_layer(h, x, w_qkvr, w_g1, w_g2, g_bias, o_norm, w_o, g_post, g_next):
    qkvr = _proj(h, w_qkvr.astype(BF16), BF16, name="gla_qkvr")
    log_a = _gla_gate(h, w_g1, w_g2, g_bias)
    o = _gla_core(qkvr, log_a, o_norm)
    return _out_proj(o, w_o.astype(BF16), jnp.zeros((D_MODEL,), F32), x, g_post, g_next, name="gla_out")


def _dwconv_ln_kernel(u_ref, uprev_ref, w_ref, b_ref, lng_ref, lnb_ref, o_ref, ext_ref, z_ref, *, tm, rc, cw):
    i = pl.program_id(0)
    ext_ref[0:C_HALO, :] = jnp.where(i > 0, uprev_ref[...], 0.0)
    ext_ref[C_HALO:C_HALO + tm, :] = u_ref[...]
    d = u_ref.shape[1]
    first = C_HALO - (C_KERNEL - 1)
    sub = 8
    for r0 in range(0, tm, rc):
        for c0 in range(0, d, cw):
            cols = slice(c0, c0 + cw)
            acc = jnp.zeros((rc, cw), F32) + b_ref[:, cols]
            for b in range(sub):
                rows = rc if b == 0 else rc + sub
                part = None
                for s in range(first, first + C_KERNEL):
                    if s % sub != b:
                        continue
                    term = w_ref[s - first:s - first + 1, cols] * ext_ref[r0 + s - b:r0 + s - b + rows, cols]
                    part = term if part is None else part + term
                if b:
                    part = pltpu.roll(part, rows - b, axis=0)[0:rc, :]
                acc = acc + part
            z_ref[r0:r0 + rc, cols] = acc
    z = z_ref[...]
    zc = z - jnp.mean(z, axis=-1, keepdims=True)
    zn = zc * lax.rsqrt(jnp.mean(zc * zc, axis=-1, keepdims=True) + EPS) * lng_ref[...] + lnb_ref[...]
    o_ref[...] = _silu(zn).astype(o_ref.dtype)


def _dwconv_ln(u, w_dw, b_dw, ln_g, ln_b, tm=128, rc=64, cw=512):
    s, d = u.shape
    row = pl.BlockSpec((1, d), lambda i: (0, 0))
    halo_blocks = tm // C_HALO
    return pl.pallas_call(
        functools.partial(_dwconv_ln_kernel, tm=tm, rc=rc, cw=cw),
        grid=(s // tm,),
        in_specs=[pl.BlockSpec((tm, d), lambda i: (i, 0)),
                  pl.BlockSpec((C_HALO, d), lambda i: (jnp.maximum(i * halo_blocks - 1, 0), 0)),
                  pl.BlockSpec((C_KERNEL, d), lambda i: (0, 0)), row, row, row],
        out_specs=pl.BlockSpec((tm, d), lambda i: (i, 0)),
        out_shape=jax.ShapeDtypeStruct((s, d), BF16),
        scratch_shapes=[pltpu.VMEM((C_HALO + tm, d), F32), pltpu.VMEM((tm, d), F32)],
        compiler_params=_params(("arbitrary",), 32),
        name="dwconv_ln",
    )(u, u, w_dw.astype(F32), _row(b_dw), _row(ln_g), _row(ln_b))


def _conformer_layer(h, x, w_pw1, b_pw1, w_dw, b_dw, ln_g, ln_b, w_pw2, b_pw2, g_post, g_next):
    u = _proj_glu(h, w_pw1.astype(BF16), b_pw1, name="conf_pw1")
    z = _dwconv_ln(u, w_dw, b_dw, ln_g, ln_b)
    return _out_proj(z, w_pw2.astype(BF16), b_pw2, x, g_post, g_next, name="conf_out")


def _sgu_kernel(u_ref, v_ref, ws_ref, bs_ref, lng_ref, lnb_ref, wout_ref, x_ref, gp_ref, gn_ref,
                xo_ref, ho_ref, p_ref, *, tm):
    v = v_ref[...]
    vc = v - jnp.mean(v, axis=-1, keepdims=True)
    vn = (vc * lax.rsqrt(jnp.mean(vc * vc, axis=-1, keepdims=True) + EPS) * lng_ref[...]
          + lnb_ref[...]).astype(BF16)
    t = D_CHUNK
    gw = D_HALF // D_GROUPS
    tril = lax.broadcasted_iota(jnp.int32, (t, t), 0) >= lax.broadcasted_iota(jnp.int32, (t, t), 1)
    for g in range(D_GROUPS):
        w = jnp.where(tril, ws_ref[g], jnp.zeros((t, t), BF16))
        cols = slice(g * gw, (g + 1) * gw)
        for c0 in range(0, tm, t):
            sv = _dot(w, vn[c0:c0 + t, cols]) + bs_ref[g]
            p_ref[c0:c0 + t, cols] = (u_ref[c0:c0 + t, cols].astype(F32) * sv).astype(BF16)
    y = _dot(p_ref[...], wout_ref[...])
    xn, hn = _residual_norm(y, x_ref[...], gp_ref[...], gn_ref[...])
    xo_ref[...] = xn
    ho_ref[...] = hn


def _sgu(u, v, w_s, b_s, ln_g, ln_b, w_out, x, g_post, g_next, tm=256):
    s, dh = u.shape
    d = w_out.shape[1]
    row = pl.BlockSpec((1, d), lambda i: (0, 0))
    rowh = pl.BlockSpec((1, dh), lambda i: (0, 0))
    return pl.pallas_call(
        functools.partial(_sgu_kernel, tm=tm),
        grid=(s // tm,),
        in_specs=[pl.BlockSpec((tm, dh), lambda i: (i, 0)), pl.BlockSpec((tm, dh), lambda i: (i, 0)),
                  pl.BlockSpec((D_GROUPS, D_CHUNK, D_CHUNK), lambda i: (0, 0, 0)),
                  pl.BlockSpec((D_GROUPS, D_CHUNK, 1), lambda i: (0, 0, 0)),
                  rowh, rowh,
                  pl.BlockSpec((dh, d), lambda i: (0, 0), pipeline_mode=pl.Buffered(1)),
                  pl.BlockSpec((tm, d), lambda i: (i, 0)), row, row],
        out_specs=[pl.BlockSpec((tm, d), lambda i: (i, 0)), pl.BlockSpec((tm, d), lambda i: (i, 0))],
        out_shape=[jax.ShapeDtypeStruct((s, d), F32), jax.ShapeDtypeStruct((s, d), BF16)],
        scratch_shapes=[pltpu.VMEM((tm, dh), BF16)],
        compiler_params=_params(("arbitrary",), 56),
        name="sgu",
    )(u, v, w_s.astype(BF16), b_s.astype(F32)[:, :, None], _row(ln_g), _row(ln_b), w_out.astype(BF16),
      x, _row(g_post), _row(g_next))


def _gmlp_layer(h, x, w_in, b_in, ln_g, ln_b, w_s, b_s, w_out, g_post, g_next):
    w_in = w_in.astype(BF16)
    u = _proj_gelu(h, w_in[:, :D_HALF], b_in[:D_HALF], BF16, name="gmlp_in_u")
    v = _proj_gelu(h, w_in[:, D_HALF:], b_in[D_HALF:], F32, name="gmlp_in_v")
    return _sgu(u, v, w_s, b_s, ln_g, ln_b, w_out, x, g_post, g_next)


def _mem_kv_kernel(mem_ref, g_ref, w_ref, o_ref):
    mem_n = (_rms(mem_ref[...]) * g_ref[...]).astype(BF16)
    o_ref[...] = _dot(mem_n, w_ref[...]).astype(o_ref.dtype)


def _mem_kv(mem, g, w_kv):
    m, d = mem.shape
    n = w_kv.shape[1]
    tn = n // 2
    return pl.pallas_call(
        _mem_kv_kernel,
        grid=(n // tn,),
        in_specs=[pl.BlockSpec((m, d), lambda j: (0, 0)), pl.BlockSpec((1, d), lambda j: (0, 0)),
                  pl.BlockSpec((d, tn), lambda j: (0, j))],
        out_specs=pl.BlockSpec((m, tn), lambda j: (0, j)),
        out_shape=jax.ShapeDtypeStruct((m, n), BF16),
        compiler_params=_params(("arbitrary",), 32),
        name="mem_kv",
    )(mem, _row(g), w_kv.astype(BF16))


def _xattn_kernel(h_ref, wq_ref, kv_ref, wo_ref, x_ref, gp_ref, gn_ref, xo_ref, ho_ref, *, n_split):
    hd = X_HEAD_DIM
    nq = X_HEADS * hd
    rows = h_ref.shape[0] // n_split
    for r0 in range(0, h_ref.shape[0], rows):
        rs = slice(r0, r0 + rows)
        q = (_dot(h_ref[rs, :], wq_ref[...]) * (hd ** -0.5)).astype(BF16)
        outs = []
        for hh in range(X_HEADS):
            cols = slice(hh * hd, (hh + 1) * hd)
            logits = _dot_nt(q[:, cols], kv_ref[:, cols])
            e = jnp.exp(logits - jnp.max(logits, axis=-1, keepdims=True))
            inv = 1.0 / jnp.sum(e, axis=-1, keepdims=True)
            o = _dot(e.astype(BF16), kv_ref[:, nq + hh * hd:nq + (hh + 1) * hd])
            outs.append((o * inv).astype(BF16))
        y = _dot(jnp.concatenate(outs, axis=1), wo_ref[...])
        xn, hn = _residual_norm(y, x_ref[rs, :], gp_ref[...], gn_ref[...])
        xo_ref[rs, :] = xn
        ho_ref[rs, :] = hn


def _xattn(h, x, w_q, kv, w_o, g_post, g_next, tm=512):
    s, d = h.shape
    nq = w_q.shape[1]
    row = pl.BlockSpec((1, d), lambda i: (0, 0))
    return pl.pallas_call(
        functools.partial(_xattn_kernel, n_split=1),
        grid=(s // tm,),
        in_specs=[pl.BlockSpec((tm, d), lambda i: (i, 0)), pl.BlockSpec((d, nq), lambda i: (0, 0)),
                  pl.BlockSpec(kv.shape, lambda i: (0, 0)), pl.BlockSpec((nq, d), lambda i: (0, 0)),
                  pl.BlockSpec((tm, d), lambda i: (i, 0)), row, row],
        out_specs=[pl.BlockSpec((tm, d), lambda i: (i, 0)), pl.BlockSpec((tm, d), lambda i: (i, 0))],
        out_shape=[jax.ShapeDtypeStruct((s, d), F32), jax.ShapeDtypeStruct((s, d), BF16)],
        compiler_params=_params(("arbitrary",), 48),
        name="xattn",
    )(h, w_q.astype(BF16), kv, w_o.astype(BF16), x, _row(g_post), _row(g_next))


def _ffn_kernel(h_ref, wg_ref, wu_ref, wc_ref, bc_ref, wd_ref, x_ref, gp_ref, *rest, tm, emit_h):
    if emit_h:
        gn_ref, xo_ref, ho_ref, acc_ref, halo_ref, ext_ref = rest
    else:
        xo_ref, acc_ref, halo_ref, ext_ref = rest
    i = pl.program_id(0)
    f = pl.program_id(1)

    @pl.when(i == 0)
    def _():
        halo_ref[f] = jnp.zeros(halo_ref.shape[1:], F32)

    @pl.when(f == 0)
    def _():
        acc_ref[...] = jnp.zeros_like(acc_ref)

    h = h_ref[...]
    gate = _dot(h, wg_ref[...])
    up = _dot(h, wu_ref[...])
    ext_ref[0:FFN_HALO, :] = halo_ref[f]
    ext_ref[FFN_HALO:FFN_HALO + tm, :] = gate
    halo_ref[f] = gate[tm - FFN_HALO:tm, :]
    conv = (wc_ref[2:3, :] * gate
            + wc_ref[1:2, :] * ext_ref[FFN_HALO - 1:FFN_HALO - 1 + tm, :]
            + wc_ref[0:1, :] * ext_ref[FFN_HALO - 2:FFN_HALO - 2 + tm, :]
            + bc_ref[...])
    act = (_gelu_tanh(conv) * up).astype(BF16)
    acc_ref[...] += _dot(act, wd_ref[...])

    @pl.when(f == pl.num_programs(1) - 1)
    def _():
        xn, hn = _residual_norm(acc_ref[...], x_ref[...], gp_ref[...], gn_ref[...] if emit_h else None)
        xo_ref[...] = xn
        if emit_h:
            ho_ref[...] = hn


def _ffn(h, x, layer, w_gate_up, w_conv, b_conv, w_down, g_post, g_next, tm=512, fc=512):
    s, d = h.shape
    ff = w_down.shape[1]
    nf = ff // fc
    emit_h = g_next is not None
    row = pl.BlockSpec((1, d), lambda i, f: (0, 0))
    tile = pl.BlockSpec((tm, d), lambda i, f: (i, 0))
    in_specs = [tile,
                pl.BlockSpec((None, d, fc), lambda i, f: (layer, 0, f)),
                pl.BlockSpec((None, d, fc), lambda i, f: (layer, 0, f + nf)),
                pl.BlockSpec((FFN_KERNEL, fc), lambda i, f: (0, f)),
                pl.BlockSpec((1, fc), lambda i, f: (0, f)),
                pl.BlockSpec((None, fc, d), lambda i, f: (layer, f, 0)),
                tile, row]
    args = [h, w_gate_up, w_gate_up, w_conv.astype(F32), _row(b_conv), w_down, x, _row(g_post)]
    out_specs = [tile]
    out_shape = [jax.ShapeDtypeStruct((s, d), F32)]
    if emit_h:
        in_specs.append(row)
        args.append(_row(g_next))
        out_specs.append(tile)
        out_shape.append(jax.ShapeDtypeStruct((s, d), BF16))
    outs = pl.pallas_call(
        functools.partial(_ffn_kernel, tm=tm, emit_h=emit_h),
        grid=(s // tm, nf),
        in_specs=in_specs,
        out_specs=out_specs,
        out_shape=out_shape,
        scratch_shapes=[pltpu.VMEM((tm, d), F32), pltpu.VMEM((nf, FFN_HALO, fc), F32),
                        pltpu.VMEM((FFN_HALO + tm, fc), F32)],
        compiler_params=_params(("arbitrary", "arbitrary"), 56),
        name="ffn",
    )(*args)
    return (outs[0], outs[1]) if emit_h else (outs[0], None)


def kernel(x, mem, norm_mix_pre, norm_mix_post, norm_mem, norm_xattn_pre, norm_xattn_post, norm_ffn_pre, norm_ffn_post, rel_bias_table, a_w_qkv, a_sinks, a_w_o, b_w_qkvr, b_w_gate1, b_w_gate2, b_gate_bias, b_o_norm, b_w_o, c_w_pw1, c_b_pw1, c_w_dw, c_b_dw, c_ln_g, c_ln_b, c_w_pw2, c_b_pw2, d_w_in, d_b_in, d_ln_g, d_ln_b, d_w_s, d_b_s, d_w_out, x_w_q, x_w_kv, x_w_o, f_w_gate_up, f_w_conv, f_b_conv, f_w_down):
    assert x.shape[0] == 1 and mem.shape[0] == 1
    xs = x[0]
    mem2 = mem[0]
    w_gate_up = f_w_gate_up.astype(BF16)
    w_down = f_w_down.astype(BF16)
    h = _rmsnorm(xs, norm_mix_pre[0])
    for i in range(DEPTH):
        kind, j = i % 4, i // 4
        g_post, g_next = norm_mix_post[i], norm_xattn_pre[i]
        if kind == 0:
            xs, h = _swa_layer(h, xs, a_w_qkv[j], a_sinks[j], a_w_o[j], rel_bias_table, g_post, g_next)
        elif kind == 1:
            xs, h = _gla_layer(h, xs, b_w_qkvr[j], b_w_gate1[j], b_w_gate2[j], b_gate_bias[j], b_o_norm[j],
                               b_w_o[j], g_post, g_next)
        elif kind == 2:
            xs, h = _conformer_layer(h, xs, c_w_pw1[j], c_b_pw1[j], c_w_dw[j], c_b_dw[j], c_ln_g[j],
                                     c_ln_b[j], c_w_pw2[j], c_b_pw2[j], g_post, g_next)
        else:
            xs, h = _gmlp_layer(h, xs, d_w_in[j], d_b_in[j], d_ln_g[j], d_ln_b[j], d_w_s[j], d_b_s[j],
                                d_w_out[j], g_post, g_next)
        kv = _mem_kv(mem2, norm_mem[i], x_w_kv[i])
        xs, h = _xattn(h, xs, x_w_q[i], kv, x_w_o[i], norm_xattn_post[i], norm_ffn_pre[i])
        g_next = norm_mix_pre[i + 1] if i + 1 < DEPTH else None
        xs, h = _ffn(h, xs, i, w_gate_up, f_w_conv[i], f_b_conv[i], w_down, norm_ffn_post[i], g_next)
    return xs[None]
```

```python
import functools
import math

import numpy as np
import jax
import jax.numpy as jnp
from jax import lax
from jax.experimental import pallas as pl
from jax.experimental.pallas import tpu as pltpu

D_MODEL = 2048
DEPTH = 4
EPS = 1e-6
NEG_INF = -1e30

A_HEADS = 32
A_KV_HEADS = 4
A_HEAD_DIM = 64
A_BLOCK = 128
REL_BUCKETS = 32
REL_MAX_EXACT = 16
REL_MAX_DIST = 128

B_HEADS = 4
B_KEY_DIM = 256
B_VAL_DIM = 512
B_GATE_RANK = 16
B_GATE_TAU = 16.0
B_CHUNK = 64

C_KERNEL = 31
C_HALO = 32

D_CHUNK = 128
D_GROUPS = 8
D_HALF = 2 * D_MODEL

X_HEADS = 4
X_HEAD_DIM = 128

FFN_DIM = 4 * D_MODEL
FFN_KERNEL = 3
FFN_HALO = 8
FFN_CHUNK = 512

LANES = 128
MIB = 1024 * 1024

BF16 = jnp.bfloat16
F32 = jnp.float32


def _params(semantics, vmem_mib):
    return pltpu.CompilerParams(dimension_semantics=semantics, vmem_limit_bytes=vmem_mib * MIB)


def _dot(a, b):
    return jnp.dot(a, b, preferred_element_type=F32)


def _dot_nt(a, b):
    return lax.dot_general(a, b, (((1,), (1,)), ((), ())), preferred_element_type=F32)


def _dot_tn(a, b):
    return lax.dot_general(a, b, (((0,), (0,)), ((), ())), preferred_element_type=F32)


def _rms(y):
    return y * lax.rsqrt(jnp.mean(y * y, axis=-1, keepdims=True) + EPS)


def _residual_norm(y, x, g_post, g_next):
    xn = x + _rms(y) * g_post
    if g_next is None:
        return xn, None
    return xn, (_rms(xn) * g_next).astype(BF16)


def _gelu_tanh(x):
    return 0.5 * x * (1.0 + jnp.tanh(math.sqrt(2.0 / math.pi) * (x + 0.044715 * (x * x * x))))


def _gelu_erf(x):
    return 0.5 * x * (1.0 + lax.erf(x * math.sqrt(0.5)))


def _silu(x):
    return x * jax.nn.sigmoid(x)


def _row(v):
    return v.reshape(1, -1).astype(F32)


def _rmsnorm_kernel(x_ref, g_ref, o_ref):
    o_ref[...] = (_rms(x_ref[...]) * g_ref[...]).astype(o_ref.dtype)


def _rmsnorm(x, g, tm=512):
    s, d = x.shape
    return pl.pallas_call(
        _rmsnorm_kernel,
        grid=(s // tm,),
        in_specs=[pl.BlockSpec((tm, d), lambda i: (i, 0)), pl.BlockSpec((1, d), lambda i: (0, 0))],
        out_specs=pl.BlockSpec((tm, d), lambda i: (i, 0)),
        out_shape=jax.ShapeDtypeStruct((s, d), BF16),
        compiler_params=_params(("arbitrary",), 32),
        name="rmsnorm",
    )(x, _row(g))


def _proj_plain_kernel(h_ref, w_ref, o_ref):
    o_ref[...] = _dot(h_ref[...], w_ref[...]).astype(o_ref.dtype)


def _proj_gelu_kernel(h_ref, w_ref, b_ref, o_ref):
    o_ref[...] = _gelu_erf(_dot(h_ref[...], w_ref[...]) + b_ref[...]).astype(o_ref.dtype)


def _proj_glu_kernel(h_ref, wa_ref, wg_ref, ba_ref, bg_ref, o_ref):
    h = h_ref[...]
    a = _dot(h, wa_ref[...]) + ba_ref[...]
    g = _dot(h, wg_ref[...]) + bg_ref[...]
    o_ref[...] = (a * jax.nn.sigmoid(g)).astype(o_ref.dtype)


def _proj(h, w, out_dtype, tm=1024, tn=1024, name="proj"):
    s, k = h.shape
    n = w.shape[1]
    tn = min(tn, n)
    return pl.pallas_call(
        _proj_plain_kernel,
        grid=(s // tm, n // tn),
        in_specs=[pl.BlockSpec((tm, k), lambda i, j: (i, 0)), pl.BlockSpec((k, tn), lambda i, j: (0, j))],
        out_specs=pl.BlockSpec((tm, tn), lambda i, j: (i, j)),
        out_shape=jax.ShapeDtypeStruct((s, n), out_dtype),
        compiler_params=_params(("arbitrary", "arbitrary"), 48),
        name=name,
    )(h, w)


def _proj_gelu(h, w, b, out_dtype, tm=1024, tn=1024, name="proj_gelu"):
    s, k = h.shape
    n = w.shape[1]
    return pl.pallas_call(
        _proj_gelu_kernel,
        grid=(s // tm, n // tn),
        in_specs=[pl.BlockSpec((tm, k), lambda i, j: (i, 0)), pl.BlockSpec((k, tn), lambda i, j: (0, j)),
                  pl.BlockSpec((1, tn), lambda i, j: (0, j))],
        out_specs=pl.BlockSpec((tm, tn), lambda i, j: (i, j)),
        out_shape=jax.ShapeDtypeStruct((s, n), out_dtype),
        compiler_params=_params(("arbitrary", "arbitrary"), 48),
        name=name,
    )(h, w, _row(b))


def _proj_glu(h, w, b, tm=1024, tn=512, name="proj_glu"):
    s, k = h.shape
    n = w.shape[1] // 2
    nb = n // tn
    b2 = _row(b)
    return pl.pallas_call(
        _proj_glu_kernel,
        grid=(s // tm, nb),
        in_specs=[pl.BlockSpec((tm, k), lambda i, j: (i, 0)),
                  pl.BlockSpec((k, tn), lambda i, j: (0, j)),
                  pl.BlockSpec((k, tn), lambda i, j: (0, j + nb)),
                  pl.BlockSpec((1, tn), lambda i, j: (0, j)),
                  pl.BlockSpec((1, tn), lambda i, j: (0, j + nb))],
        out_specs=pl.BlockSpec((tm, tn), lambda i, j: (i, j)),
        out_shape=jax.ShapeDtypeStruct((s, n), F32),
        compiler_params=_params(("arbitrary", "arbitrary"), 48),
        name=name,
    )(h, w, w, b2, b2)


def _out_proj_kernel(a_ref, w_ref, b_ref, x_ref, gp_ref, gn_ref, xo_ref, ho_ref, *, n_split):
    rows = a_ref.shape[0] // n_split
    for r0 in range(0, a_ref.shape[0], rows):
        rs = slice(r0, r0 + rows)
        y = _dot(a_ref[rs, :], w_ref[...]) + b_ref[...]
        xn, hn = _residual_norm(y, x_ref[rs, :], gp_ref[...], gn_ref[...])
        xo_ref[rs, :] = xn
        ho_ref[rs, :] = hn


def _out_proj(a, w, b, x, g_post, g_next, tm=512, name="out_proj"):
    s, k = a.shape
    d = w.shape[1]
    row = pl.BlockSpec((1, d), lambda i: (0, 0))
    return pl.pallas_call(
        functools.partial(_out_proj_kernel, n_split=4),
        grid=(s // tm,),
        in_specs=[pl.BlockSpec((tm, k), lambda i: (i, 0)), pl.BlockSpec((k, d), lambda i: (0, 0)), row,
                  pl.BlockSpec((tm, d), lambda i: (i, 0)), row, row],
        out_specs=[pl.BlockSpec((tm, d), lambda i: (i, 0)), pl.BlockSpec((tm, d), lambda i: (i, 0))],
        out_shape=[jax.ShapeDtypeStruct((s, d), F32), jax.ShapeDtypeStruct((s, d), BF16)],
        compiler_params=_params(("arbitrary",), 56),
        name=name,
    )(a, w, _row(b), x, _row(g_post), _row(g_next))


def _t5_bucket(dist):
    n = np.maximum(dist, 0)
    large = REL_MAX_EXACT + (np.log(np.maximum(n, 1) / REL_MAX_EXACT)
                             / math.log(REL_MAX_DIST / REL_MAX_EXACT)
                             * (REL_BUCKETS - REL_MAX_EXACT)).astype(np.int32)
    large = np.minimum(large, REL_BUCKETS - 1)
    return np.where(n < REL_MAX_EXACT, n, large).astype(np.int32)


def _swa_geometry():
    qi = np.arange(A_BLOCK)[:, None]
    kj = np.arange(2 * A_BLOCK)[None, :]
    dist = qi + A_BLOCK - kj
    in_window = ((dist >= 0) & (dist < A_BLOCK)).astype(np.float32)
    return _t5_bucket(dist), np.tile(in_window, (A_HEADS // A_KV_HEADS // 2, 2))


def _swa_bias_kernel(tab_ref, bucket_ref, o_ref):
    pair = pl.program_id(0)
    bucket = bucket_ref[...]
    for half in range(2):
        head = 2 * pair + half
        acc = jnp.zeros(bucket.shape, F32)
        for b in range(REL_BUCKETS):
            acc = jnp.where(bucket == b, tab_ref[b, head], acc)
        o_ref[0, :, half * 2 * A_BLOCK:(half + 1) * 2 * A_BLOCK] = acc


def _swa_bias(rel_table, bucket):
    return pl.pallas_call(
        _swa_bias_kernel,
        grid=(A_HEADS // 2,),
        in_specs=[pl.BlockSpec(memory_space=pltpu.SMEM),
                  pl.BlockSpec((A_BLOCK, 2 * A_BLOCK), lambda p: (0, 0))],
        out_specs=pl.BlockSpec((1, A_BLOCK, 4 * A_BLOCK), lambda p: (p, 0, 0)),
        out_shape=jax.ShapeDtypeStruct((A_HEADS // 2, A_BLOCK, 4 * A_BLOCK), F32),
        compiler_params=_params(("arbitrary",), 16),
        name="swa_bias",
    )(rel_table.astype(F32), jnp.asarray(bucket))


def _swa_kernel(sink_ref, q_ref, kvp_ref, kvc_ref, bias_ref, win_ref, o_ref):
    n = pl.program_id(0)
    hd = A_HEAD_DIM
    nk = 2 * A_BLOCK
    npair = A_HEADS // A_KV_HEADS // 2
    rows = npair * A_BLOCK
    first_k = lax.broadcasted_iota(jnp.int32, (nk, 2 * hd), 1) < hd
    col = lax.broadcasted_iota(jnp.int32, (rows, 2 * nk), 1)
    key_valid = (n > 0) | ((col & (nk - 1)) >= A_BLOCK)
    mask = (win_ref[...] > 0.0) & key_valid
    first_o = lax.broadcasted_iota(jnp.int32, (rows, 2 * hd), 1) < hd
    row_pair = lax.broadcasted_iota(jnp.int32, (rows, 1), 0) // A_BLOCK
    kv_cols = A_KV_HEADS * 2 * hd
    for kvh in range(A_KV_HEADS):
        ks = slice(kvh * 2 * hd, (kvh + 1) * 2 * hd)
        vs = slice(kv_cols + kvh * 2 * hd, kv_cols + (kvh + 1) * 2 * hd)
        kk = jnp.concatenate([kvp_ref[:, ks], kvc_ref[:, ks]], axis=0)
        vv = jnp.concatenate([kvp_ref[:, vs], kvc_ref[:, vs]], axis=0)
        zero = jnp.zeros_like(kk)
        k_diag = jnp.concatenate([jnp.where(first_k, kk, zero), jnp.where(first_k, zero, kk)], axis=0)
        v_diag = jnp.concatenate([jnp.where(first_k, vv, zero), jnp.where(first_k, zero, vv)], axis=0)
        p0 = kvh * npair
        pair_cols = [slice((p0 + j) * 2 * hd, (p0 + j + 1) * 2 * hd) for j in range(npair)]
        q4 = jnp.concatenate([q_ref[:, c] for c in pair_cols], axis=0) * (hd ** -0.5)
        logits = _dot_nt(q4, k_diag)
        bias = bias_ref[p0:p0 + npair].reshape(rows, 2 * nk)
        logits = jnp.where(mask, logits + bias, NEG_INF)
        es, invs = [], []
        for half in range(2):
            lg = logits[:, half * nk:(half + 1) * nk]
            sink = jnp.full((rows, 1), sink_ref[2 * p0 + half], F32)
            for j in range(1, npair):
                sink = jnp.where(row_pair >= j, sink_ref[2 * (p0 + j) + half], sink)
            m = jnp.maximum(jnp.max(lg, axis=-1, keepdims=True), sink)
            e = jnp.exp(lg - m)
            den = jnp.sum(e, axis=-1, keepdims=True) + jnp.exp(sink - m)
            es.append(e.astype(BF16))
            invs.append(1.0 / den)
        o4 = _dot(jnp.concatenate(es, axis=1), v_diag)
        o4 = (o4 * jnp.where(first_o, invs[0], invs[1])).astype(o_ref.dtype)
        for j in range(npair):
            o_ref[:, pair_cols[j]] = o4[j * A_BLOCK:(j + 1) * A_BLOCK, :]


def _swa_attention(qkv, sinks, bias, window):
    s = qkv.shape[0]
    nq = A_HEADS * A_HEAD_DIM
    nkv = 2 * A_KV_HEADS * 2 * A_HEAD_DIM
    kv_blk = nq // nkv
    return pl.pallas_call(
        _swa_kernel,
        grid=(s // A_BLOCK,),
        in_specs=[pl.BlockSpec(memory_space=pltpu.SMEM),
                  pl.BlockSpec((A_BLOCK, nq), lambda n: (n, 0)),
                  pl.BlockSpec((A_BLOCK, nkv), lambda n: (jnp.maximum(n - 1, 0), kv_blk)),
                  pl.BlockSpec((A_BLOCK, nkv), lambda n: (n, kv_blk)),
                  pl.BlockSpec((A_HEADS // 2, A_BLOCK, 4 * A_BLOCK), lambda n: (0, 0, 0)),
                  pl.BlockSpec(window.shape, lambda n: (0, 0))],
        out_specs=pl.BlockSpec((A_BLOCK, nq), lambda n: (n, 0)),
        out_shape=jax.ShapeDtypeStruct((s, nq), BF16),
        compiler_params=_params(("arbitrary",), 32),
        name="swa_attention",
    )(sinks.astype(F32), qkv, qkv, qkv, bias, window)


def _swa_layer(h, x, w_qkv, sinks, w_o, rel_table, g_post, g_next):
    nq = A_HEADS * A_HEAD_DIM
    nkv = A_KV_HEADS * A_HEAD_DIM
    wq = w_qkv[:, :nq]
    wk = w_qkv[:, nq:nq + nkv].reshape(D_MODEL, A_KV_HEADS, 1, A_HEAD_DIM)
    wv = w_qkv[:, nq + nkv:].reshape(D_MODEL, A_KV_HEADS, 1, A_HEAD_DIM)
    dup = lambda w: jnp.broadcast_to(w, (D_MODEL, A_KV_HEADS, 2, A_HEAD_DIM)).reshape(D_MODEL, 2 * nkv)
    w_ext = jnp.concatenate([wq, dup(wk), dup(wv)], axis=1).astype(BF16)
    qkv = _proj(h, w_ext, BF16, name="swa_qkv")
    bucket, window = _swa_geometry()
    bias = _swa_bias(rel_table, bucket)
    o = _swa_attention(qkv, sinks, bias, jnp.asarray(window))
    return _out_proj(o, w_o.astype(BF16), jnp.zeros((D_MODEL,), F32), x, g_post, g_next, name="swa_out")


def _gla_gate_kernel(h_ref, w1_ref, w2_ref, b_ref, o_ref):
    t = _dot(h_ref[...], w1_ref[...]).astype(BF16)
    gk = _dot(t, w2_ref[...]) + b_ref[...]
    log_sig = -(jnp.maximum(-gk, 0.0) + jnp.log1p(jnp.exp(-jnp.abs(gk))))
    o_ref[...] = log_sig / B_GATE_TAU


def _gla_gate(h, w1, w2, b, tm=512):
    s, d = h.shape
    n = w2.shape[1]
    w1p = jnp.zeros((d, LANES), BF16).at[:, :B_GATE_RANK].set(w1.astype(BF16))
    w2p = jnp.zeros((LANES, n), BF16).at[:B_GATE_RANK, :].set(w2.astype(BF16))
    return pl.pallas_call(
        _gla_gate_kernel,
        grid=(s // tm,),
        in_specs=[pl.BlockSpec((tm, d), lambda i: (i, 0)), pl.BlockSpec((d, LANES), lambda i: (0, 0)),
                  pl.BlockSpec((LANES, n), lambda i: (0, 0)), pl.BlockSpec((1, n), lambda i: (0, 0))],
        out_specs=pl.BlockSpec((tm, n), lambda i: (i, 0)),
        out_shape=jax.ShapeDtypeStruct((s, n), F32),
        compiler_params=_params(("arbitrary",), 32),
        name="gla_gate",
    )(h, w1p, w2p, _row(b))


def _gla_kernel(qkvr_ref, la_ref, onorm_ref, o_ref, st_ref, *, n_sub):
    @pl.when(pl.program_id(0) == 0)
    def _():
        st_ref[...] = jnp.zeros_like(st_ref)

    c = B_CHUNK
    dk_all = B_HEADS * B_KEY_DIM
    dv_all = B_HEADS * B_VAL_DIM
    causal = (lax.broadcasted_iota(jnp.int32, (c, c), 0) >= lax.broadcasted_iota(jnp.int32, (c, c), 1))
    tri = causal.astype(BF16)

    def chunk(ci, carry):
        rows = pl.ds(pl.multiple_of(ci * c, c), c)
        la = la_ref[rows, :]
        la_hi = la.astype(BF16)
        la_lo = (la - la_hi.astype(F32)).astype(BF16)
        cum = _dot(tri, la_hi) + _dot(tri, la_lo)
        last = cum[c - 1:c, :]
        q = qkvr_ref[rows, 0:dk_all].astype(F32) * (B_KEY_DIM ** -0.5)
        k = qkvr_ref[rows, dk_all:2 * dk_all].astype(F32)
        q_dec = (q * jnp.exp(cum)).astype(BF16)
        k_inv = (k * jnp.exp(-cum)).astype(BF16)
        k_end = (k * jnp.exp(last - cum)).astype(BF16)
        decay = jnp.exp(last)
        for hh in range(B_HEADS):
            ks = slice(hh * B_KEY_DIM, (hh + 1) * B_KEY_DIM)
            v = qkvr_ref[rows, 2 * dk_all + hh * B_VAL_DIM:2 * dk_all + (hh + 1) * B_VAL_DIM]
            r = qkvr_ref[rows, 2 * dk_all + dv_all + hh * B_VAL_DIM:
                         2 * dk_all + dv_all + (hh + 1) * B_VAL_DIM].astype(F32)
            att = jnp.where(causal, _dot_nt(q_dec[:, ks], k_inv[:, ks]), 0.0).astype(BF16)
            st = st_ref[hh]
            o = _dot(att, v) + _dot_nt(q_dec[:, ks], st.astype(BF16))
            st_ref[hh] = st * decay[:, ks] + _dot_tn(v, k_end[:, ks])
            o = _rms(o) * onorm_ref[...]
            o_ref[rows, hh * B_VAL_DIM:(hh + 1) * B_VAL_DIM] = (o * _silu(r)).astype(o_ref.dtype)
        return carry

    lax.fori_loop(0, n_sub, chunk, 0)


def _gla_core(qkvr, log_a, o_norm, tb=256):
    s = qkvr.shape[0]
    dv_all = B_HEADS * B_VAL_DIM
    return pl.pallas_call(
        functools.partial(_gla_kernel, n_sub=tb // B_CHUNK),
        grid=(s // tb,),
        in_specs=[pl.BlockSpec((tb, qkvr.shape[1]), lambda i: (i, 0)),
                  pl.BlockSpec((tb, log_a.shape[1]), lambda i: (i, 0)),
                  pl.BlockSpec((1, B_VAL_DIM), lambda i: (0, 0))],
        out_specs=pl.BlockSpec((tb, dv_all), lambda i: (i, 0)),
        out_shape=jax.ShapeDtypeStruct((s, dv_all), BF16),
        scratch_shapes=[pltpu.VMEM((B_HEADS, B_VAL_DIM, B_KEY_DIM), F32)],
        compiler_params=_params(("arbitrary",), 32),
        name="gla_core",
    )(qkvr, log_a, _row(o_norm))


def _gla_layer(h, x, w_qkvr, w_g1, w_g2, g_bias, o_norm, w_o, g_post, g_next):
    qkvr = _proj(h, w_qkvr.astype(BF16), BF16, name="gla_qkvr")
    log_a = _gla_gate(h, w_g1, w_g2, g_bias)
    o = _gla_core(qkvr, log_a, o_norm)
    return _out_proj(o, w_o.astype(BF16), jnp.zeros((D_MODEL,), F32), x, g_post, g_next, name="gla_out")


def _dwconv_ln_kernel(u_ref, uprev_ref, w_ref, b_ref, lng_ref, lnb_ref, o_ref, ext_ref, z_ref, *, tm, rc, cw):
    i = pl.program_id(0)
    ext_ref[0:C_HALO, :] = jnp.where(i > 0, uprev_ref[...], 0.0)
    ext_ref[C_HALO:C_HALO + tm, :] = u_ref[...]
    d = u_ref.shape[1]
    first = C_HALO - (C_KERNEL - 1)
    sub = 8
    for r0 in range(0, tm, rc):
        for c0 in range(0, d, cw):
            cols = slice(c0, c0 + cw)
            acc = jnp.zeros((rc, cw), F32) + b_ref[:, cols]
            for b in range(sub):
                rows = rc if b == 0 else rc + sub
                part = None
                for s in range(first, first + C_KERNEL):
                    if s % sub != b:
                        continue
                    term = w_ref[s - first:s - first + 1, cols] * ext_ref[r0 + s - b:r0 + s - b + rows, cols]
                    part = term if part is None else part + term
                if b:
                    part = pltpu.roll(part, rows - b, axis=0)[0:rc, :]
                acc = acc + part
            z_ref[r0:r0 + rc, cols] = acc
    z = z_ref[...]
    zc = z - jnp.mean(z, axis=-1, keepdims=True)
    zn = zc * lax.rsqrt(jnp.mean(zc * zc, axis=-1, keepdims=True) + EPS) * lng_ref[...] + lnb_ref[...]
    o_ref[...] = _silu(zn).astype(o_ref.dtype)


def _dwconv_ln(u, w_dw, b_dw, ln_g, ln_b, tm=128, rc=128, cw=128):
    s, d = u.shape
    row = pl.BlockSpec((1, d), lambda i: (0, 0))
    halo_blocks = tm // C_HALO
    return pl.pallas_call(
        functools.partial(_dwconv_ln_kernel, tm=tm, rc=rc, cw=cw),
        grid=(s // tm,),
        in_specs=[pl.BlockSpec((tm, d), lambda i: (i, 0)),
                  pl.BlockSpec((C_HALO, d), lambda i: (jnp.maximum(i * halo_blocks - 1, 0), 0)),
                  pl.BlockSpec((C_KERNEL, d), lambda i: (0, 0)), row, row, row],
        out_specs=pl.BlockSpec((tm, d), lambda i: (i, 0)),
        out_shape=jax.ShapeDtypeStruct((s, d), BF16),
        scratch_shapes=[pltpu.VMEM((C_HALO + tm, d), F32), pltpu.VMEM((tm, d), F32)],
        compiler_params=_params(("arbitrary",), 32),
        name="dwconv_ln",
    )(u, u, w_dw.astype(F32), _row(b_dw), _row(ln_g), _row(ln_b))


def _conformer_layer(h, x, w_pw1, b_pw1, w_dw, b_dw, ln_g, ln_b, w_pw2, b_pw2, g_post, g_next):
    u = _proj_glu(h, w_pw1.astype(BF16), b_pw1, name="conf_pw1")
    z = _dwconv_ln(u, w_dw, b_dw, ln_g, ln_b)
    return _out_proj(z, w_pw2.astype(BF16), b_pw2, x, g_post, g_next, name="conf_out")


def _sgu_kernel(u_ref, v_ref, ws_ref, bs_ref, lng_ref, lnb_ref, wout_ref, x_ref, gp_ref, gn_ref,
                xo_ref, ho_ref, p_ref, *, tm):
    t = D_CHUNK
    gw = D_HALF // D_GROUPS
    tril = lax.broadcasted_iota(jnp.int32, (t, t), 0) >= lax.broadcasted_iota(jnp.int32, (t, t), 1)
    for c0 in range(0, tm, t):
        rs = slice(c0, c0 + t)
        v = v_ref[rs, :]
        vc = v - jnp.mean(v, axis=-1, keepdims=True)
        vn = (vc * lax.rsqrt(jnp.mean(vc * vc, axis=-1, keepdims=True) + EPS) * lng_ref[...]
              + lnb_ref[...]).astype(BF16)
        for g in range(D_GROUPS):
            w = jnp.where(tril, ws_ref[g], jnp.zeros((t, t), BF16))
            cols = slice(g * gw, (g + 1) * gw)
            sv = _dot(w, vn[:, cols]) + bs_ref[g]
            p_ref[rs, cols] = (u_ref[rs, cols].astype(F32) * sv).astype(BF16)
        y = _dot(p_ref[rs, :], wout_ref[...])
        xn, hn = _residual_norm(y, x_ref[rs, :], gp_ref[...], gn_ref[...])
        xo_ref[rs, :] = xn
        ho_ref[rs, :] = hn


def _sgu(u, v, w_s, b_s, ln_g, ln_b, w_out, x, g_post, g_next, tm=256):
    s, dh = u.shape
    d = w_out.shape[1]
    row = pl.BlockSpec((1, d), lambda i: (0, 0))
    rowh = pl.BlockSpec((1, dh), lambda i: (0, 0))
    return pl.pallas_call(
        functools.partial(_sgu_kernel, tm=tm),
        grid=(s // tm,),
        in_specs=[pl.BlockSpec((tm, dh), lambda i: (i, 0)), pl.BlockSpec((tm, dh), lambda i: (i, 0)),
                  pl.BlockSpec((D_GROUPS, D_CHUNK, D_CHUNK), lambda i: (0, 0, 0)),
                  pl.BlockSpec((D_GROUPS, D_CHUNK, 1), lambda i: (0, 0, 0)),
                  rowh, rowh,
                  pl.BlockSpec((dh, d), lambda i: (0, 0), pipeline_mode=pl.Buffered(1)),
                  pl.BlockSpec((tm, d), lambda i: (i, 0)), row, row],
        out_specs=[pl.BlockSpec((tm, d), lambda i: (i, 0)), pl.BlockSpec((tm, d), lambda i: (i, 0))],
        out_shape=[jax.ShapeDtypeStruct((s, d), F32), jax.ShapeDtypeStruct((s, d), BF16)],
        scratch_shapes=[pltpu.VMEM((tm, dh), BF16)],
        compiler_params=_params(("arbitrary",), 56),
        name="sgu",
    )(u, v, w_s.astype(BF16), b_s.astype(F32)[:, :, None], _row(ln_g), _row(ln_b), w_out.astype(BF16),
      x, _row(g_post), _row(g_next))


def _gmlp_layer(h, x, w_in, b_in, ln_g, ln_b, w_s, b_s, w_out, g_post, g_next):
    w_in = w_in.astype(BF16)
    u = _proj_gelu(h, w_in[:, :D_HALF], b_in[:D_HALF], BF16, name="gmlp_in_u")
    v = _proj_gelu(h, w_in[:, D_HALF:], b_in[D_HALF:], F32, name="gmlp_in_v")
    return _sgu(u, v, w_s, b_s, ln_g, ln_b, w_out, x, g_post, g_next)


def _mem_kv_kernel(mem_ref, g_ref, w_ref, o_ref):
    mem_n = (_rms(mem_ref[...]) * g_ref[...]).astype(BF16)
    o_ref[...] = _dot(mem_n, w_ref[...]).astype(o_ref.dtype)


def _mem_kv(mem, g, w_kv):
    m, d = mem.shape
    n = w_kv.shape[1]
    tn = n // 2
    return pl.pallas_call(
        _mem_kv_kernel,
        grid=(n // tn,),
        in_specs=[pl.BlockSpec((m, d), lambda j: (0, 0)), pl.BlockSpec((1, d), lambda j: (0, 0)),
                  pl.BlockSpec((d, tn), lambda j: (0, j))],
        out_specs=pl.BlockSpec((m, tn), lambda j: (0, j)),
        out_shape=jax.ShapeDtypeStruct((m, n), BF16),
        compiler_params=_params(("arbitrary",), 32),
        name="mem_kv",
    )(mem, _row(g), w_kv.astype(BF16))


def _xattn_kernel(h_ref, wq_ref, kv_ref, wo_ref, x_ref, gp_ref, gn_ref, xo_ref, ho_ref, *, n_split):
    hd = X_HEAD_DIM
    nq = X_HEADS * hd
    rows = h_ref.shape[0] // n_split
    for r0 in range(0, h_ref.shape[0], rows):
        rs = slice(r0, r0 + rows)
        q = (_dot(h_ref[rs, :], wq_ref[...]) * (hd ** -0.5)).astype(BF16)
        outs = []
        for hh in range(X_HEADS):
            cols = slice(hh * hd, (hh + 1) * hd)
            logits = _dot_nt(q[:, cols], kv_ref[:, cols])
            e = jnp.exp(logits - jnp.max(logits, axis=-1, keepdims=True))
            inv = 1.0 / jnp.sum(e, axis=-1, keepdims=True)
            o = _dot(e.astype(BF16), kv_ref[:, nq + hh * hd:nq + (hh + 1) * hd])
            outs.append((o * inv).astype(BF16))
        y = _dot(jnp.concatenate(outs, axis=1), wo_ref[...])
        xn, hn = _residual_norm(y, x_ref[rs, :], gp_ref[...], gn_ref[...])
        xo_ref[rs, :] = xn
        ho_ref[rs, :] = hn


def _xattn(h, x, w_q, kv, w_o, g_post, g_next, tm=512):
    s, d = h.shape
    nq = w_q.shape[1]
    row = pl.BlockSpec((1, d), lambda i: (0, 0))
    return pl.pallas_call(
        functools.partial(_xattn_kernel, n_split=1),
        grid=(s // tm,),
        in_specs=[pl.BlockSpec((tm, d), lambda i: (i, 0)), pl.BlockSpec((d, nq), lambda i: (0, 0)),
                  pl.BlockSpec(kv.shape, lambda i: (0, 0)), pl.BlockSpec((nq, d), lambda i: (0, 0)),
                  pl.BlockSpec((tm, d), lambda i: (i, 0)), row, row],
        out_specs=[pl.BlockSpec((tm, d), lambda i: (i, 0)), pl.BlockSpec((tm, d), lambda i: (i, 0))],
        out_shape=[jax.ShapeDtypeStruct((s, d), F32), jax.ShapeDtypeStruct((s, d), BF16)],
        compiler_params=_params(("arbitrary",), 48),
        name="xattn",
    )(h, w_q.astype(BF16), kv, w_o.astype(BF16), x, _row(g_post), _row(g_next))


def _ffn_kernel(h_ref, wgu_ref, wcb_ref, wd_ref, x_ref, gp_ref, *rest, tm, fc, emit_h):
    if emit_h:
        gn_ref, xo_ref, ho_ref, acc_ref, halo_ref, ext_ref = rest
    else:
        xo_ref, acc_ref, halo_ref, ext_ref = rest
    i = pl.program_id(0)
    f = pl.program_id(1)
    wc_ref = wcb_ref.at[f]

    @pl.when(i == 0)
    def _():
        halo_ref[f] = jnp.zeros(halo_ref.shape[1:], F32)

    @pl.when(f == 0)
    def _():
        acc_ref[...] = jnp.zeros_like(acc_ref)

    h = h_ref[...]
    gate = _dot(h, wgu_ref[:, 0:fc])
    up = _dot(h, wgu_ref[:, fc:2 * fc])
    ext_ref[0:FFN_HALO, :] = halo_ref[f]
    ext_ref[FFN_HALO:FFN_HALO + tm, :] = gate
    halo_ref[f] = gate[tm - FFN_HALO:tm, :]
    conv = (wc_ref[2:3, :] * gate
            + wc_ref[1:2, :] * ext_ref[FFN_HALO - 1:FFN_HALO - 1 + tm, :]
            + wc_ref[0:1, :] * ext_ref[FFN_HALO - 2:FFN_HALO - 2 + tm, :]
            + wc_ref[FFN_KERNEL:FFN_KERNEL + 1, :])
    act = (_gelu_tanh(conv) * up).astype(BF16)
    acc_ref[...] += _dot(act, wd_ref[...])

    @pl.when(f == pl.num_programs(1) - 1)
    def _():
        xn, hn = _residual_norm(acc_ref[...], x_ref[...], gp_ref[...], gn_ref[...] if emit_h else None)
        xo_ref[...] = xn
        if emit_h:
            ho_ref[...] = hn


def _ffn_chunked_gate_up(w_gate_up, fc=FFN_CHUNK):
    nl, d, n2 = w_gate_up.shape
    nf = n2 // 2 // fc
    w = w_gate_up.astype(BF16).reshape(nl, d, 2, nf, fc)
    return jnp.transpose(w, (0, 3, 1, 2, 4)).reshape(nl, nf, d, 2 * fc)


def _ffn(h, x, layer, w_gate_up, w_conv, b_conv, w_down, g_post, g_next, tm=512):
    s, d = h.shape
    _, nf, _, fc2 = w_gate_up.shape
    fc = fc2 // 2
    emit_h = g_next is not None
    row = pl.BlockSpec((1, d), lambda i, f: (0, 0))
    tile = pl.BlockSpec((tm, d), lambda i, f: (i, 0))
    wcb = jnp.concatenate([w_conv.astype(F32), _row(b_conv)], axis=0).reshape(FFN_KERNEL + 1, nf, fc)
    wcb = jnp.transpose(wcb, (1, 0, 2))
    in_specs = [tile,
                pl.BlockSpec((None, None, d, fc2), lambda i, f: (layer, f, 0, 0)),
                pl.BlockSpec(wcb.shape, lambda i, f: (0, 0, 0)),
                pl.BlockSpec((None, fc, d), lambda i, f: (layer, f, 0)),
                tile, row]
    args = [h, w_gate_up, wcb, w_down, x, _row(g_post)]
    out_specs = [tile]
    out_shape = [jax.ShapeDtypeStruct((s, d), F32)]
    if emit_h:
        in_specs.append(row)
        args.append(_row(g_next))
        out_specs.append(tile)
        out_shape.append(jax.ShapeDtypeStruct((s, d), BF16))
    outs = pl.pallas_call(
        functools.partial(_ffn_kernel, tm=tm, fc=fc, emit_h=emit_h),
        grid=(s // tm, nf),
        in_specs=in_specs,
        out_specs=out_specs,
        out_shape=out_shape,
        scratch_shapes=[pltpu.VMEM((tm, d), F32), pltpu.VMEM((nf, FFN_HALO, fc), F32),
                        pltpu.VMEM((FFN_HALO + tm, fc), F32)],
        compiler_params=_params(("arbitrary", "arbitrary"), 56),
        name="ffn",
    )(*args)
    return (outs[0], outs[1]) if emit_h else (outs[0], None)


def kernel(x, mem, norm_mix_pre, norm_mix_post, norm_mem, norm_xattn_pre, norm_xattn_post, norm_ffn_pre, norm_ffn_post, rel_bias_table, a_w_qkv, a_sinks, a_w_o, b_w_qkvr, b_w_gate1, b_w_gate2, b_gate_bias, b_o_norm, b_w_o, c_w_pw1, c_b_pw1, c_w_dw, c_b_dw, c_ln_g, c_ln_b, c_w_pw2, c_b_pw2, d_w_in, d_b_in, d_ln_g, d_ln_b, d_w_s, d_b_s, d_w_out, x_w_q, x_w_kv, x_w_o, f_w_gate_up, f_w_conv, f_b_conv, f_w_down):
    assert x.shape[0] == 1 and mem.shape[0] == 1
    xs = x[0]
    mem2 = mem[0]
    w_gate_up = _ffn_chunked_gate_up(f_w_gate_up)
    w_down = f_w_down.astype(BF16)
    h = _rmsnorm(xs, norm_mix_pre[0])
    for i in range(DEPTH):
        kind, j = i % 4, i // 4
        g_post, g_next = norm_mix_post[i], norm_xattn_pre[i]
        if kind == 0:
            xs, h = _swa_layer(h, xs, a_w_qkv[j], a_sinks[j], a_w_o[j], rel_bias_table, g_post, g_next)
        elif kind == 1:
            xs, h = _gla_layer(h, xs, b_w_qkvr[j], b_w_gate1[j], b_w_gate2[j], b_gate_bias[j], b_o_norm[j],
                               b_w_o[j], g_post, g_next)
        elif kind == 2:
            xs, h = _conformer_layer(h, xs, c_w_pw1[j], c_b_pw1[j], c_w_dw[j], c_b_dw[j], c_ln_g[j],
                                     c_ln_b[j], c_w_pw2[j], c_b_pw2[j], g_post, g_next)
        else:
            xs, h = _gmlp_layer(h, xs, d_w_in[j], d_b_in[j], d_ln_g[j], d_ln_b[j], d_w_s[j], d_b_s[j],
                                d_w_out[j], g_post, g_next)
        kv = _mem_kv(mem2, norm_mem[i], x_w_kv[i])
        xs, h = _xattn(h, xs, x_w_q[i], kv, x_w_o[i], norm_xattn_post[i], norm_ffn_pre[i])
        g_next = norm_mix_pre[i + 1] if i + 1 < DEPTH else None
        xs, h = _ffn(h, xs, i, w_gate_up, f_w_conv[i], f_b_conv[i], w_down, norm_ffn_post[i], g_next)
    return xs[None]
```

```python
import functools
import math

import numpy as np
import jax
import jax.numpy as jnp
from jax import lax
from jax.experimental import pallas as pl
from jax.experimental.pallas import tpu as pltpu

D_MODEL = 2048
DEPTH = 4
EPS = 1e-6
NEG_INF = -1e30

A_HEADS = 32
A_KV_HEADS = 4
A_HEAD_DIM = 64
A_BLOCK = 128
REL_BUCKETS = 32
REL_MAX_EXACT = 16
REL_MAX_DIST = 128

B_HEADS = 4
B_KEY_DIM = 256
B_VAL_DIM = 512
B_GATE_RANK = 16
B_GATE_TAU = 16.0
B_CHUNK = 64

C_KERNEL = 31
C_HALO = 32

D_CHUNK = 128
D_GROUPS = 8
D_HALF = 2 * D_MODEL

X_HEADS = 4
X_HEAD_DIM = 128

FFN_DIM = 4 * D_MODEL
FFN_KERNEL = 3
FFN_HALO = 8
FFN_CHUNK = 512

LANES = 128
MIB = 1024 * 1024

BF16 = jnp.bfloat16
F32 = jnp.float32


def _params(semantics, vmem_mib):
    return pltpu.CompilerParams(dimension_semantics=semantics, vmem_limit_bytes=vmem_mib * MIB)


def _dot(a, b):
    return jnp.dot(a, b, preferred_element_type=F32)


def _dot_nt(a, b):
    return lax.dot_general(a, b, (((1,), (1,)), ((), ())), preferred_element_type=F32)


def _dot_tn(a, b):
    return lax.dot_general(a, b, (((0,), (0,)), ((), ())), preferred_element_type=F32)


def _rms(y):
    return y * lax.rsqrt(jnp.mean(y * y, axis=-1, keepdims=True) + EPS)


def _residual_norm(y, x, g_post, g_next):
    xn = x + _rms(y) * g_post
    if g_next is None:
        return xn, None
    return xn, (_rms(xn) * g_next).astype(BF16)


def _gelu_tanh(x):
    return 0.5 * x * (1.0 + jnp.tanh(math.sqrt(2.0 / math.pi) * (x + 0.044715 * (x * x * x))))


def _gelu_erf(x):
    return 0.5 * x * (1.0 + lax.erf(x * math.sqrt(0.5)))


def _silu(x):
    return x * jax.nn.sigmoid(x)


def _row(v):
    return v.reshape(1, -1).astype(F32)


def _rmsnorm_kernel(x_ref, g_ref, o_ref):
    o_ref[...] = (_rms(x_ref[...]) * g_ref[...]).astype(o_ref.dtype)


def _rmsnorm(x, g, tm=512):
    s, d = x.shape
    return pl.pallas_call(
        _rmsnorm_kernel,
        grid=(s // tm,),
        in_specs=[pl.BlockSpec((tm, d), lambda i: (i, 0)), pl.BlockSpec((1, d), lambda i: (0, 0))],
        out_specs=pl.BlockSpec((tm, d), lambda i: (i, 0)),
        out_shape=jax.ShapeDtypeStruct((s, d), BF16),
        compiler_params=_params(("arbitrary",), 32),
        name="rmsnorm",
    )(x, _row(g))


def _proj_plain_kernel(h_ref, w_ref, o_ref):
    o_ref[...] = _dot(h_ref[...], w_ref[...]).astype(o_ref.dtype)


def _proj_gelu_kernel(h_ref, w_ref, b_ref, o_ref):
    o_ref[...] = _gelu_erf(_dot(h_ref[...], w_ref[...]) + b_ref[...]).astype(o_ref.dtype)


def _proj_glu_kernel(h_ref, wa_ref, wg_ref, ba_ref, bg_ref, o_ref):
    h = h_ref[...]
    a = _dot(h, wa_ref[...]) + ba_ref[...]
    g = _dot(h, wg_ref[...]) + bg_ref[...]
    o_ref[...] = (a * jax.nn.sigmoid(g)).astype(o_ref.dtype)


def _proj(h, w, out_dtype, tm=1024, tn=1024, name="proj"):
    s, k = h.shape
    n = w.shape[1]
    tn = min(tn, n)
    return pl.pallas_call(
        _proj_plain_kernel,
        grid=(s // tm, n // tn),
        in_specs=[pl.BlockSpec((tm, k), lambda i, j: (i, 0)), pl.BlockSpec((k, tn), lambda i, j: (0, j))],
        out_specs=pl.BlockSpec((tm, tn), lambda i, j: (i, j)),
        out_shape=jax.ShapeDtypeStruct((s, n), out_dtype),
        compiler_params=_params(("arbitrary", "arbitrary"), 48),
        name=name,
    )(h, w)


def _proj_gelu(h, w, b, out_dtype, tm=1024, tn=1024, name="proj_gelu"):
    s, k = h.shape
    n = w.shape[1]
    return pl.pallas_call(
        _proj_gelu_kernel,
        grid=(s // tm, n // tn),
        in_specs=[pl.BlockSpec((tm, k), lambda i, j: (i, 0)), pl.BlockSpec((k, tn), lambda i, j: (0, j)),
                  pl.BlockSpec((1, tn), lambda i, j: (0, j))],
        out_specs=pl.BlockSpec((tm, tn), lambda i, j: (i, j)),
        out_shape=jax.ShapeDtypeStruct((s, n), out_dtype),
        compiler_params=_params(("arbitrary", "arbitrary"), 48),
        name=name,
    )(h, w, _row(b))


def _proj_glu(h, w, b, tm=1024, tn=512, name="proj_glu"):
    s, k = h.shape
    n = w.shape[1] // 2
    nb = n // tn
    b2 = _row(b)
    return pl.pallas_call(
        _proj_glu_kernel,
        grid=(s // tm, nb),
        in_specs=[pl.BlockSpec((tm, k), lambda i, j: (i, 0)),
                  pl.BlockSpec((k, tn), lambda i, j: (0, j)),
                  pl.BlockSpec((k, tn), lambda i, j: (0, j + nb)),
                  pl.BlockSpec((1, tn), lambda i, j: (0, j)),
                  pl.BlockSpec((1, tn), lambda i, j: (0, j + nb))],
        out_specs=pl.BlockSpec((tm, tn), lambda i, j: (i, j)),
        out_shape=jax.ShapeDtypeStruct((s, n), F32),
        compiler_params=_params(("arbitrary", "arbitrary"), 48),
        name=name,
    )(h, w, w, b2, b2)


def _out_proj_kernel(a_ref, w_ref, b_ref, x_ref, gp_ref, gn_ref, xo_ref, ho_ref, *, n_split):
    rows = a_ref.shape[0] // n_split
    for r0 in range(0, a_ref.shape[0], rows):
        rs = slice(r0, r0 + rows)
        y = _dot(a_ref[rs, :], w_ref[...]) + b_ref[...]
        xn, hn = _residual_norm(y, x_ref[rs, :], gp_ref[...], gn_ref[...])
        xo_ref[rs, :] = xn
        ho_ref[rs, :] = hn


def _out_proj(a, w, b, x, g_post, g_next, tm=512, name="out_proj"):
    s, k = a.shape
    d = w.shape[1]
    row = pl.BlockSpec((1, d), lambda i: (0, 0))
    return pl.pallas_call(
        functools.partial(_out_proj_kernel, n_split=4),
        grid=(s // tm,),
        in_specs=[pl.BlockSpec((tm, k), lambda i: (i, 0)), pl.BlockSpec((k, d), lambda i: (0, 0)), row,
                  pl.BlockSpec((tm, d), lambda i: (i, 0)), row, row],
        out_specs=[pl.BlockSpec((tm, d), lambda i: (i, 0)), pl.BlockSpec((tm, d), lambda i: (i, 0))],
        out_shape=[jax.ShapeDtypeStruct((s, d), F32), jax.ShapeDtypeStruct((s, d), BF16)],
        compiler_params=_params(("arbitrary",), 56),
        name=name,
    )(a, w, _row(b), x, _row(g_post), _row(g_next))


def _t5_bucket(dist):
    n = np.maximum(dist, 0)
    large = REL_MAX_EXACT + (np.log(np.maximum(n, 1) / REL_MAX_EXACT)
                             / math.log(REL_MAX_DIST / REL_MAX_EXACT)
                             * (REL_BUCKETS - REL_MAX_EXACT)).astype(np.int32)
    large = np.minimum(large, REL_BUCKETS - 1)
    return np.where(n < REL_MAX_EXACT, n, large).astype(np.int32)


def _swa_geometry():
    qi = np.arange(A_BLOCK)[:, None]
    kj = np.arange(2 * A_BLOCK)[None, :]
    dist = qi + A_BLOCK - kj
    in_window = ((dist >= 0) & (dist < A_BLOCK)).astype(np.float32)
    return _t5_bucket(dist), np.tile(in_window, (A_HEADS // A_KV_HEADS // 2, 2))


def _swa_bias_kernel(tab_ref, bucket_ref, o_ref):
    pair = pl.program_id(0)
    bucket = bucket_ref[...]
    for half in range(2):
        head = 2 * pair + half
        acc = jnp.zeros(bucket.shape, F32)
        for b in range(REL_BUCKETS):
            acc = jnp.where(bucket == b, tab_ref[b, head], acc)
        o_ref[0, :, half * 2 * A_BLOCK:(half + 1) * 2 * A_BLOCK] = acc


def _swa_bias(rel_table, bucket):
    return pl.pallas_call(
        _swa_bias_kernel,
        grid=(A_HEADS // 2,),
        in_specs=[pl.BlockSpec(memory_space=pltpu.SMEM),
                  pl.BlockSpec((A_BLOCK, 2 * A_BLOCK), lambda p: (0, 0))],
        out_specs=pl.BlockSpec((1, A_BLOCK, 4 * A_BLOCK), lambda p: (p, 0, 0)),
        out_shape=jax.ShapeDtypeStruct((A_HEADS // 2, A_BLOCK, 4 * A_BLOCK), F32),
        compiler_params=_params(("arbitrary",), 16),
        name="swa_bias",
    )(rel_table.astype(F32), jnp.asarray(bucket))


def _swa_kernel(sink_ref, q_ref, kvp_ref, kvc_ref, bias_ref, win_ref, o_ref):
    n = pl.program_id(0)
    hd = A_HEAD_DIM
    nk = 2 * A_BLOCK
    npair = A_HEADS // A_KV_HEADS // 2
    rows = npair * A_BLOCK
    first_k = lax.broadcasted_iota(jnp.int32, (nk, 2 * hd), 1) < hd
    col = lax.broadcasted_iota(jnp.int32, (rows, 2 * nk), 1)
    key_valid = (n > 0) | ((col & (nk - 1)) >= A_BLOCK)
    mask = (win_ref[...] > 0.0) & key_valid
    first_o = lax.broadcasted_iota(jnp.int32, (rows, 2 * hd), 1) < hd
    row_pair = lax.broadcasted_iota(jnp.int32, (rows, 1), 0) // A_BLOCK
    kv_cols = A_KV_HEADS * 2 * hd
    for kvh in range(A_KV_HEADS):
        ks = slice(kvh * 2 * hd, (kvh + 1) * 2 * hd)
        vs = slice(kv_cols + kvh * 2 * hd, kv_cols + (kvh + 1) * 2 * hd)
        kk = jnp.concatenate([kvp_ref[:, ks], kvc_ref[:, ks]], axis=0)
        vv = jnp.concatenate([kvp_ref[:, vs], kvc_ref[:, vs]], axis=0)
        zero = jnp.zeros_like(kk)
        k_diag = jnp.concatenate([jnp.where(first_k, kk, zero), jnp.where(first_k, zero, kk)], axis=0)
        v_diag = jnp.concatenate([jnp.where(first_k, vv, zero), jnp.where(first_k, zero, vv)], axis=0)
        p0 = kvh * npair
        pair_cols = [slice((p0 + j) * 2 * hd, (p0 + j + 1) * 2 * hd) for j in range(npair)]
        q4 = jnp.concatenate([q_ref[:, c] for c in pair_cols], axis=0) * (hd ** -0.5)
        logits = _dot_nt(q4, k_diag)
        bias = bias_ref[p0:p0 + npair].reshape(rows, 2 * nk)
        logits = jnp.where(mask, logits + bias, NEG_INF)
        es, invs = [], []
        for half in range(2):
            lg = logits[:, half * nk:(half + 1) * nk]
            sink = jnp.full((rows, 1), sink_ref[2 * p0 + half], F32)
            for j in range(1, npair):
                sink = jnp.where(row_pair >= j, sink_ref[2 * (p0 + j) + half], sink)
            m = jnp.maximum(jnp.max(lg, axis=-1, keepdims=True), sink)
            e = jnp.exp(lg - m)
            den = jnp.sum(e, axis=-1, keepdims=True) + jnp.exp(sink - m)
            es.append(e.astype(BF16))
            invs.append(1.0 / den)
        o4 = _dot(jnp.concatenate(es, axis=1), v_diag)
        o4 = (o4 * jnp.where(first_o, invs[0], invs[1])).astype(o_ref.dtype)
        for j in range(npair):
            o_ref[:, pair_cols[j]] = o4[j * A_BLOCK:(j + 1) * A_BLOCK, :]


def _swa_attention(qkv, sinks, bias, window):
    s = qkv.shape[0]
    nq = A_HEADS * A_HEAD_DIM
    nkv = 2 * A_KV_HEADS * 2 * A_HEAD_DIM
    kv_blk = nq // nkv
    return pl.pallas_call(
        _swa_kernel,
        grid=(s // A_BLOCK,),
        in_specs=[pl.BlockSpec(memory_space=pltpu.SMEM),
                  pl.BlockSpec((A_BLOCK, nq), lambda n: (n, 0)),
                  pl.BlockSpec((A_BLOCK, nkv), lambda n: (jnp.maximum(n - 1, 0), kv_blk)),
                  pl.BlockSpec((A_BLOCK, nkv), lambda n: (n, kv_blk)),
                  pl.BlockSpec((A_HEADS // 2, A_BLOCK, 4 * A_BLOCK), lambda n: (0, 0, 0)),
                  pl.BlockSpec(window.shape, lambda n: (0, 0))],
        out_specs=pl.BlockSpec((A_BLOCK, nq), lambda n: (n, 0)),
        out_shape=jax.ShapeDtypeStruct((s, nq), BF16),
        compiler_params=_params(("arbitrary",), 32),
        name="swa_attention",
    )(sinks.astype(F32), qkv, qkv, qkv, bias, window)


def _swa_layer(h, x, w_qkv, sinks, w_o, rel_table, g_post, g_next):
    nq = A_HEADS * A_HEAD_DIM
    nkv = A_KV_HEADS * A_HEAD_DIM
    wq = w_qkv[:, :nq]
    wk = w_qkv[:, nq:nq + nkv].reshape(D_MODEL, A_KV_HEADS, 1, A_HEAD_DIM)
    wv = w_qkv[:, nq + nkv:].reshape(D_MODEL, A_KV_HEADS, 1, A_HEAD_DIM)
    dup = lambda w: jnp.broadcast_to(w, (D_MODEL, A_KV_HEADS, 2, A_HEAD_DIM)).reshape(D_MODEL, 2 * nkv)
    w_ext = jnp.concatenate([wq, dup(wk), dup(wv)], axis=1).astype(BF16)
    qkv = _proj(h, w_ext, BF16, name="swa_qkv")
    bucket, window = _swa_geometry()
    bias = _swa_bias(rel_table, bucket)
    o = _swa_attention(qkv, sinks, bias, jnp.asarray(window))
    return _out_proj(o, w_o.astype(BF16), jnp.zeros((D_MODEL,), F32), x, g_post, g_next, name="swa_out")


def _gla_gate_kernel(h_ref, w1_ref, w2_ref, b_ref, o_ref):
    t = _dot(h_ref[...], w1_ref[...]).astype(BF16)
    gk = _dot(t, w2_ref[...]) + b_ref[...]
    log_sig = -(jnp.maximum(-gk, 0.0) + jnp.log1p(jnp.exp(-jnp.abs(gk))))
    o_ref[...] = log_sig / B_GATE_TAU


def _gla_gate(h, w1, w2, b, tm=512):
    s, d = h.shape
    n = w2.shape[1]
    w1p = jnp.zeros((d, LANES), BF16).at[:, :B_GATE_RANK].set(w1.astype(BF16))
    w2p = jnp.zeros((LANES, n), BF16).at[:B_GATE_RANK, :].set(w2.astype(BF16))
    return pl.pallas_call(
        _gla_gate_kernel,
        grid=(s // tm,),
        in_specs=[pl.BlockSpec((tm, d), lambda i: (i, 0)), pl.BlockSpec((d, LANES), lambda i: (0, 0)),
                  pl.BlockSpec((LANES, n), lambda i: (0, 0)), pl.BlockSpec((1, n), lambda i: (0, 0))],
        out_specs=pl.BlockSpec((tm, n), lambda i: (i, 0)),
        out_shape=jax.ShapeDtypeStruct((s, n), F32),
        compiler_params=_params(("arbitrary",), 32),
        name="gla_gate",
    )(h, w1p, w2p, _row(b))


def _gla_kernel(qkvr_ref, la_ref, onorm_ref, o_ref, st_ref, *, n_sub):
    @pl.when(pl.program_id(0) == 0)
    def _():
        st_ref[...] = jnp.zeros_like(st_ref)

    c = B_CHUNK
    dk_all = B_HEADS * B_KEY_DIM
    dv_all = B_HEADS * B_VAL_DIM
    causal = (lax.broadcasted_iota(jnp.int32, (c, c), 0) >= lax.broadcasted_iota(jnp.int32, (c, c), 1))
    tri = causal.astype(BF16)

    def chunk(ci, carry):
        rows = pl.ds(pl.multiple_of(ci * c, c), c)
        la = la_ref[rows, :]
        la_hi = la.astype(BF16)
        la_lo = (la - la_hi.astype(F32)).astype(BF16)
        cum = _dot(tri, la_hi) + _dot(tri, la_lo)
        last = cum[c - 1:c, :]
        q = qkvr_ref[rows, 0:dk_all].astype(F32) * (B_KEY_DIM ** -0.5)
        k = qkvr_ref[rows, dk_all:2 * dk_all].astype(F32)
        q_dec = (q * jnp.exp(cum)).astype(BF16)
        k_inv = (k * jnp.exp(-cum)).astype(BF16)
        k_end = (k * jnp.exp(last - cum)).astype(BF16)
        decay = jnp.exp(last)
        for hh in range(B_HEADS):
            ks = slice(hh * B_KEY_DIM, (hh + 1) * B_KEY_DIM)
            v = qkvr_ref[rows, 2 * dk_all + hh * B_VAL_DIM:2 * dk_all + (hh + 1) * B_VAL_DIM]
            r = qkvr_ref[rows, 2 * dk_all + dv_all + hh * B_VAL_DIM:
                         2 * dk_all + dv_all + (hh + 1) * B_VAL_DIM].astype(F32)
            att = jnp.where(causal, _dot_nt(q_dec[:, ks], k_inv[:, ks]), 0.0).astype(BF16)
            st = st_ref[hh]
            o = _dot(att, v) + _dot_nt(q_dec[:, ks], st.astype(BF16))
            st_ref[hh] = st * decay[:, ks] + _dot_tn(v, k_end[:, ks])
            o = _rms(o) * onorm_ref[...]
            o_ref[rows, hh * B_VAL_DIM:(hh + 1) * B_VAL_DIM] = (o * _silu(r)).astype(o_ref.dtype)
        return carry

    lax.fori_loop(0, n_sub, chunk, 0)


def _gla_core(qkvr, log_a, o_norm, tb=256):
    s = qkvr.shape[0]
    dv_all = B_HEADS * B_VAL_DIM
    return pl.pallas_call(
        functools.partial(_gla_kernel, n_sub=tb // B_CHUNK),
        grid=(s // tb,),
        in_specs=[pl.BlockSpec((tb, qkvr.shape[1]), lambda i: (i, 0)),
                  pl.BlockSpec((tb, log_a.shape[1]), lambda i: (i, 0)),
                  pl.BlockSpec((1, B_VAL_DIM), lambda i: (0, 0))],
        out_specs=pl.BlockSpec((tb, dv_all), lambda i: (i, 0)),
        out_shape=jax.ShapeDtypeStruct((s, dv_all), BF16),
        scratch_shapes=[pltpu.VMEM((B_HEADS, B_VAL_DIM, B_KEY_DIM), F32)],
        compiler_params=_params(("arbitrary",), 32),
        name="gla_core",
    )(qkvr, log_a, _row(o_norm))


def _gla_layer(h, x, w_qkvr, w_g1, w_g2, g_bias, o_norm, w_o, g_post, g_next):
    qkvr = _proj(h, w_qkvr.astype(BF16), BF16, name="gla_qkvr")
    log_a = _gla_gate(h, w_g1, w_g2, g_bias)
    o = _gla_core(qkvr, log_a, o_norm)
    return _out_proj(o, w_o.astype(BF16), jnp.zeros((D_MODEL,), F32), x, g_post, g_next, name="gla_out")


def _dwconv_ln_kernel(u_ref, uprev_ref, w_ref, b_ref, lng_ref, lnb_ref, o_ref, ext_ref, z_ref, *, tm, rc, cw):
    i = pl.program_id(0)
    ext_ref[0:C_HALO, :] = jnp.where(i > 0, uprev_ref[...], 0.0)
    ext_ref[C_HALO:C_HALO + tm, :] = u_ref[...]
    d = u_ref.shape[1]
    first = C_HALO - (C_KERNEL - 1)
    sub = 8
    for r0 in range(0, tm, rc):
        for c0 in range(0, d, cw):
            cols = slice(c0, c0 + cw)
            acc = jnp.zeros((rc, cw), F32) + b_ref[:, cols]
            for b in range(sub):
                rows = rc if b == 0 else rc + sub
                part = None
                for s in range(first, first + C_KERNEL):
                    if s % sub != b:
                        continue
                    term = w_ref[s - first:s - first + 1, cols] * ext_ref[r0 + s - b:r0 + s - b + rows, cols]
                    part = term if part is None else part + term
                if b:
                    part = pltpu.roll(part, rows - b, axis=0)[0:rc, :]
                acc = acc + part
            z_ref[r0:r0 + rc, cols] = acc
    z = z_ref[...]
    zc = z - jnp.mean(z, axis=-1, keepdims=True)
    zn = zc * lax.rsqrt(jnp.mean(zc * zc, axis=-1, keepdims=True) + EPS) * lng_ref[...] + lnb_ref[...]
    o_ref[...] = _silu(zn).astype(o_ref.dtype)


def _dwconv_ln(u, w_dw, b_dw, ln_g, ln_b, tm=128, rc=128, cw=128):
    s, d = u.shape
    row = pl.BlockSpec((1, d), lambda i: (0, 0))
    halo_blocks = tm // C_HALO
    return pl.pallas_call(
        functools.partial(_dwconv_ln_kernel, tm=tm, rc=rc, cw=cw),
        grid=(s // tm,),
        in_specs=[pl.BlockSpec((tm, d), lambda i: (i, 0)),
                  pl.BlockSpec((C_HALO, d), lambda i: (jnp.maximum(i * halo_blocks - 1, 0), 0)),
                  pl.BlockSpec((C_KERNEL, d), lambda i: (0, 0)), row, row, row],
        out_specs=pl.BlockSpec((tm, d), lambda i: (i, 0)),
        out_shape=jax.ShapeDtypeStruct((s, d), BF16),
        scratch_shapes=[pltpu.VMEM((C_HALO + tm, d), F32), pltpu.VMEM((tm, d), F32)],
        compiler_params=_params(("arbitrary",), 32),
        name="dwconv_ln",
    )(u, u, w_dw.astype(F32), _row(b_dw), _row(ln_g), _row(ln_b))


def _conformer_layer(h, x, w_pw1, b_pw1, w_dw, b_dw, ln_g, ln_b, w_pw2, b_pw2, g_post, g_next):
    u = _proj_glu(h, w_pw1.astype(BF16), b_pw1, name="conf_pw1")
    z = _dwconv_ln(u, w_dw, b_dw, ln_g, ln_b)
    return _out_proj(z, w_pw2.astype(BF16), b_pw2, x, g_post, g_next, name="conf_out")


def _sgu_kernel(u_ref, v_ref, ws_ref, bs_ref, lng_ref, lnb_ref, wout_ref, x_ref, gp_ref, gn_ref,
                xo_ref, ho_ref, p_ref, *, tm):
    t = D_CHUNK
    gw = D_HALF // D_GROUPS
    tril = lax.broadcasted_iota(jnp.int32, (t, t), 0) >= lax.broadcasted_iota(jnp.int32, (t, t), 1)
    for c0 in range(0, tm, t):
        rs = slice(c0, c0 + t)
        v = v_ref[rs, :]
        vc = v - jnp.mean(v, axis=-1, keepdims=True)
        vn = (vc * lax.rsqrt(jnp.mean(vc * vc, axis=-1, keepdims=True) + EPS) * lng_ref[...]
              + lnb_ref[...]).astype(BF16)
        for g in range(D_GROUPS):
            w = jnp.where(tril, ws_ref[g], jnp.zeros((t, t), BF16))
            cols = slice(g * gw, (g + 1) * gw)
            sv = _dot(w, vn[:, cols]) + bs_ref[g]
            p_ref[rs, cols] = (u_ref[rs, cols].astype(F32) * sv).astype(BF16)
        y = _dot(p_ref[rs, :], wout_ref[...])
        xn, hn = _residual_norm(y, x_ref[rs, :], gp_ref[...], gn_ref[...])
        xo_ref[rs, :] = xn
        ho_ref[rs, :] = hn


def _sgu(u, v, w_s, b_s, ln_g, ln_b, w_out, x, g_post, g_next, tm=256):
    s, dh = u.shape
    d = w_out.shape[1]
    row = pl.BlockSpec((1, d), lambda i: (0, 0))
    rowh = pl.BlockSpec((1, dh), lambda i: (0, 0))
    return pl.pallas_call(
        functools.partial(_sgu_kernel, tm=tm),
        grid=(s // tm,),
        in_specs=[pl.BlockSpec((tm, dh), lambda i: (i, 0)), pl.BlockSpec((tm, dh), lambda i: (i, 0)),
                  pl.BlockSpec((D_GROUPS, D_CHUNK, D_CHUNK), lambda i: (0, 0, 0)),
                  pl.BlockSpec((D_GROUPS, D_CHUNK, 1), lambda i: (0, 0, 0)),
                  rowh, rowh,
                  pl.BlockSpec((dh, d), lambda i: (0, 0), pipeline_mode=pl.Buffered(1)),
                  pl.BlockSpec((tm, d), lambda i: (i, 0)), row, row],
        out_specs=[pl.BlockSpec((tm, d), lambda i: (i, 0)), pl.BlockSpec((tm, d), lambda i: (i, 0))],
        out_shape=[jax.ShapeDtypeStruct((s, d), F32), jax.ShapeDtypeStruct((s, d), BF16)],
        scratch_shapes=[pltpu.VMEM((tm, dh), BF16)],
        compiler_params=_params(("arbitrary",), 56),
        name="sgu",
    )(u, v, w_s.astype(BF16), b_s.astype(F32)[:, :, None], _row(ln_g), _row(ln_b), w_out.astype(BF16),
      x, _row(g_post), _row(g_next))


def _gmlp_layer(h, x, w_in, b_in, ln_g, ln_b, w_s, b_s, w_out, g_post, g_next):
    w_in = w_in.astype(BF16)
    u = _proj_gelu(h, w_in[:, :D_HALF], b_in[:D_HALF], BF16, name="gmlp_in_u")
    v = _proj_gelu(h, w_in[:, D_HALF:], b_in[D_HALF:], F32, name="gmlp_in_v")
    return _sgu(u, v, w_s, b_s, ln_g, ln_b, w_out, x, g_post, g_next)


def _mem_kv_kernel(mem_ref, g_ref, w_ref, o_ref):
    mem_n = (_rms(mem_ref[...]) * g_ref[...]).astype(BF16)
    o_ref[...] = _dot(mem_n, w_ref[...]).astype(o_ref.dtype)


def _mem_kv(mem, g, w_kv):
    m, d = mem.shape
    n = w_kv.shape[1]
    tn = n // 2
    return pl.pallas_call(
        _mem_kv_kernel,
        grid=(n // tn,),
        in_specs=[pl.BlockSpec((m, d), lambda j: (0, 0)), pl.BlockSpec((1, d), lambda j: (0, 0)),
                  pl.BlockSpec((d, tn), lambda j: (0, j))],
        out_specs=pl.BlockSpec((m, tn), lambda j: (0, j)),
        out_shape=jax.ShapeDtypeStruct((m, n), BF16),
        compiler_params=_params(("arbitrary",), 32),
        name="mem_kv",
    )(mem, _row(g), w_kv.astype(BF16))


def _xattn_kernel(h_ref, wq_ref, kv_ref, wo_ref, x_ref, gp_ref, gn_ref, xo_ref, ho_ref, *, n_split):
    hd = X_HEAD_DIM
    nq = X_HEADS * hd
    rows = h_ref.shape[0] // n_split
    for r0 in range(0, h_ref.shape[0], rows):
        rs = slice(r0, r0 + rows)
        q = (_dot(h_ref[rs, :], wq_ref[...]) * (hd ** -0.5)).astype(BF16)
        outs = []
        for hh in range(X_HEADS):
            cols = slice(hh * hd, (hh + 1) * hd)
            logits = _dot_nt(q[:, cols], kv_ref[:, cols])
            e = jnp.exp(logits - jnp.max(logits, axis=-1, keepdims=True))
            inv = 1.0 / jnp.sum(e, axis=-1, keepdims=True)
            o = _dot(e.astype(BF16), kv_ref[:, nq + hh * hd:nq + (hh + 1) * hd])
            outs.append((o * inv).astype(BF16))
        y = _dot(jnp.concatenate(outs, axis=1), wo_ref[...])
        xn, hn = _residual_norm(y, x_ref[rs, :], gp_ref[...], gn_ref[...])
        xo_ref[rs, :] = xn
        ho_ref[rs, :] = hn


def _xattn(h, x, w_q, kv, w_o, g_post, g_next, tm=512):
    s, d = h.shape
    nq = w_q.shape[1]
    row = pl.BlockSpec((1, d), lambda i: (0, 0))
    return pl.pallas_call(
        functools.partial(_xattn_kernel, n_split=1),
        grid=(s // tm,),
        in_specs=[pl.BlockSpec((tm, d), lambda i: (i, 0)), pl.BlockSpec((d, nq), lambda i: (0, 0)),
                  pl.BlockSpec(kv.shape, lambda i: (0, 0)), pl.BlockSpec((nq, d), lambda i: (0, 0)),
                  pl.BlockSpec((tm, d), lambda i: (i, 0)), row, row],
        out_specs=[pl.BlockSpec((tm, d), lambda i: (i, 0)), pl.BlockSpec((tm, d), lambda i: (i, 0))],
        out_shape=[jax.ShapeDtypeStruct((s, d), F32), jax.ShapeDtypeStruct((s, d), BF16)],
        compiler_params=_params(("arbitrary",), 48),
        name="xattn",
    )(h, w_q.astype(BF16), kv, w_o.astype(BF16), x, _row(g_post), _row(g_next))


def _ffn_kernel(h_ref, wg_ref, wu_ref, wc_ref, bc_ref, wd_ref, x_ref, gp_ref, *rest, tm, emit_h):
    if emit_h:
        gn_ref, xo_ref, ho_ref, halo_ref, ext_ref = rest
    else:
        xo_ref, halo_ref, ext_ref = rest
    i = pl.program_id(0)
    f = pl.program_id(1)

    @pl.when(i == 0)
    def _():
        halo_ref[f] = jnp.zeros(halo_ref.shape[1:], F32)

    @pl.when(f == 0)
    def _():
        xo_ref[...] = jnp.zeros_like(xo_ref)

    h = h_ref[...]
    gate = _dot(h, wg_ref[...])
    up = _dot(h, wu_ref[...])
    ext_ref[0:FFN_HALO, :] = halo_ref[f]
    ext_ref[FFN_HALO:FFN_HALO + tm, :] = gate
    halo_ref[f] = gate[tm - FFN_HALO:tm, :]
    conv = (wc_ref[2:3, :] * gate
            + wc_ref[1:2, :] * ext_ref[FFN_HALO - 1:FFN_HALO - 1 + tm, :]
            + wc_ref[0:1, :] * ext_ref[FFN_HALO - 2:FFN_HALO - 2 + tm, :]
            + bc_ref[...])
    act = (_gelu_tanh(conv) * up).astype(BF16)
    xo_ref[...] += _dot(act, wd_ref[...])

    @pl.when(f == pl.num_programs(1) - 1)
    def _():
        xn, hn = _residual_norm(xo_ref[...], x_ref[...], gp_ref[...], gn_ref[...] if emit_h else None)
        xo_ref[...] = xn
        if emit_h:
            ho_ref[...] = hn


def _ffn(h, x, layer, w_gate_up, w_conv, b_conv, w_down, g_post, g_next, tm=1024, fc=FFN_CHUNK):
    s, d = h.shape
    ff = w_down.shape[1]
    nf = ff // fc
    emit_h = g_next is not None
    row = pl.BlockSpec((1, d), lambda i, f: (0, 0))
    tile = pl.BlockSpec((tm, d), lambda i, f: (i, 0))
    x_tile = pl.BlockSpec((tm, d), lambda i, f: (i, 0), pipeline_mode=pl.Buffered(1))
    in_specs = [tile,
                pl.BlockSpec((None, d, fc), lambda i, f: (layer, 0, f)),
                pl.BlockSpec((None, d, fc), lambda i, f: (layer, 0, f + nf)),
                pl.BlockSpec((FFN_KERNEL, fc), lambda i, f: (0, f)),
                pl.BlockSpec((1, fc), lambda i, f: (0, f)),
                pl.BlockSpec((None, fc, d), lambda i, f: (layer, f, 0)),
                x_tile, row]
    args = [h, w_gate_up, w_gate_up, w_conv.astype(F32), _row(b_conv), w_down, x, _row(g_post)]
    out_specs = [tile]
    out_shape = [jax.ShapeDtypeStruct((s, d), F32)]
    if emit_h:
        in_specs.append(row)
        args.append(_row(g_next))
        out_specs.append(tile)
        out_shape.append(jax.ShapeDtypeStruct((s, d), BF16))
    outs = pl.pallas_call(
        functools.partial(_ffn_kernel, tm=tm, emit_h=emit_h),
        grid=(s // tm, nf),
        in_specs=in_specs,
        out_specs=out_specs,
        out_shape=out_shape,
        scratch_shapes=[pltpu.VMEM((nf, FFN_HALO, fc), F32), pltpu.VMEM((FFN_HALO + tm, fc), F32)],
        compiler_params=_params(("arbitrary", "arbitrary"), 60),
        name="ffn",
    )(*args)
    return (outs[0], outs[1]) if emit_h else (outs[0], None)


def kernel(x, mem, norm_mix_pre, norm_mix_post, norm_mem, norm_xattn_pre, norm_xattn_post, norm_ffn_pre, norm_ffn_post, rel_bias_table, a_w_qkv, a_sinks, a_w_o, b_w_qkvr, b_w_gate1, b_w_gate2, b_gate_bias, b_o_norm, b_w_o, c_w_pw1, c_b_pw1, c_w_dw, c_b_dw, c_ln_g, c_ln_b, c_w_pw2, c_b_pw2, d_w_in, d_b_in, d_ln_g, d_ln_b, d_w_s, d_b_s, d_w_out, x_w_q, x_w_kv, x_w_o, f_w_gate_up, f_w_conv, f_b_conv, f_w_down):
    assert x.shape[0] == 1 and mem.shape[0] == 1
    xs = x[0]
    mem2 = mem[0]
    w_gate_up = f_w_gate_up.astype(BF16)
    w_down = f_w_down.astype(BF16)
    h = _rmsnorm(xs, norm_mix_pre[0])
    for i in range(DEPTH):
        kind, j = i % 4, i // 4
        g_post, g_next = norm_mix_post[i], norm_xattn_pre[i]
        if kind == 0:
            xs, h = _swa_layer(h, xs, a_w_qkv[j], a_sinks[j], a_w_o[j], rel_bias_table, g_post, g_next)
        elif kind == 1:
            xs, h = _gla_layer(h, xs, b_w_qkvr[j], b_w_gate1[j], b_w_gate2[j], b_gate_bias[j], b_o_norm[j],
                               b_w_o[j], g_post, g_next)
        elif kind == 2:
            xs, h = _conformer_layer(h, xs, c_w_pw1[j], c_b_pw1[j], c_w_dw[j], c_b_dw[j], c_ln_g[j],
                                     c_ln_b[j], c_w_pw2[j], c_b_pw2[j], g_post, g_next)
        else:
            xs, h = _gmlp_layer(h, xs, d_w_in[j], d_b_in[j], d_ln_g[j], d_ln_b[j], d_w_s[j], d_b_s[j],
                                d_w_out[j], g_post, g_next)
        kv = _mem_kv(mem2, norm_mem[i], x_w_kv[i])
        xs, h = _xattn(h, xs, x_w_q[i], kv, x_w_o[i], norm_xattn_post[i], norm_ffn_pre[i])
        g_next = norm_mix_pre[i + 1] if i + 1 < DEPTH else None
        xs, h = _ffn(h, xs, i, w_gate_up, f_w_conv[i], f_b_conv[i], w_down, norm_ffn_post[i], g_next)
    return xs[None]
```

```python
import functools
import math

import numpy as np
import jax
import jax.numpy as jnp
from jax import lax
from jax.experimental import pallas as pl
from jax.experimental.pallas import tpu as pltpu

D_MODEL = 2048
DEPTH = 4
EPS = 1e-6
NEG_INF = -1e30

A_HEADS = 32
A_KV_HEADS = 4
A_HEAD_DIM = 64
A_BLOCK = 128
REL_BUCKETS = 32
REL_MAX_EXACT = 16
REL_MAX_DIST = 128

B_HEADS = 4
B_KEY_DIM = 256
B_VAL_DIM = 512
B_GATE_RANK = 16
B_GATE_TAU = 16.0
B_CHUNK = 64

C_KERNEL = 31
C_HALO = 32

D_CHUNK = 128
D_GROUPS = 8
D_HALF = 2 * D_MODEL

X_HEADS = 4
X_HEAD_DIM = 128

FFN_DIM = 4 * D_MODEL
FFN_KERNEL = 3
FFN_HALO = 8
FFN_CHUNK = 512

LANES = 128
MIB = 1024 * 1024

BF16 = jnp.bfloat16
F32 = jnp.float32


def _params(semantics, vmem_mib):
    return pltpu.CompilerParams(dimension_semantics=semantics, vmem_limit_bytes=vmem_mib * MIB)


def _dot(a, b):
    return jnp.dot(a, b, preferred_element_type=F32)


def _dot_nt(a, b):
    return lax.dot_general(a, b, (((1,), (1,)), ((), ())), preferred_element_type=F32)


def _dot_tn(a, b):
    return lax.dot_general(a, b, (((0,), (0,)), ((), ())), preferred_element_type=F32)


def _rms(y):
    return y * lax.rsqrt(jnp.mean(y * y, axis=-1, keepdims=True) + EPS)


def _residual_norm(y, x, g_post, g_next):
    xn = x + _rms(y) * g_post
    if g_next is None:
        return xn, None
    return xn, (_rms(xn) * g_next).astype(BF16)


def _gelu_tanh(x):
    return 0.5 * x * (1.0 + jnp.tanh(math.sqrt(2.0 / math.pi) * (x + 0.044715 * (x * x * x))))


def _gelu_erf(x):
    return 0.5 * x * (1.0 + lax.erf(x * math.sqrt(0.5)))


def _silu(x):
    return x * jax.nn.sigmoid(x)


def _row(v):
    return v.reshape(1, -1).astype(F32)


def _rmsnorm_kernel(x_ref, g_ref, o_ref):
    o_ref[...] = (_rms(x_ref[...]) * g_ref[...]).astype(o_ref.dtype)


def _rmsnorm(x, g, tm=512):
    s, d = x.shape
    return pl.pallas_call(
        _rmsnorm_kernel,
        grid=(s // tm,),
        in_specs=[pl.BlockSpec((tm, d), lambda i: (i, 0)), pl.BlockSpec((1, d), lambda i: (0, 0))],
        out_specs=pl.BlockSpec((tm, d), lambda i: (i, 0)),
        out_shape=jax.ShapeDtypeStruct((s, d), BF16),
        compiler_params=_params(("arbitrary",), 32),
        name="rmsnorm",
    )(x, _row(g))


def _proj_plain_kernel(h_ref, w_ref, o_ref):
    o_ref[...] = _dot(h_ref[...], w_ref[...]).astype(o_ref.dtype)


def _proj_gelu_kernel(h_ref, w_ref, b_ref, o_ref):
    o_ref[...] = _gelu_erf(_dot(h_ref[...], w_ref[...]) + b_ref[...]).astype(o_ref.dtype)


def _proj_glu_kernel(h_ref, wa_ref, wg_ref, ba_ref, bg_ref, o_ref):
    h = h_ref[...]
    a = _dot(h, wa_ref[...]) + ba_ref[...]
    g = _dot(h, wg_ref[...]) + bg_ref[...]
    o_ref[...] = (a * jax.nn.sigmoid(g)).astype(o_ref.dtype)


def _proj(h, w, out_dtype, tm=1024, tn=1024, name="proj"):
    s, k = h.shape
    n = w.shape[1]
    tn = min(tn, n)
    return pl.pallas_call(
        _proj_plain_kernel,
        grid=(s // tm, n // tn),
        in_specs=[pl.BlockSpec((tm, k), lambda i, j: (i, 0)), pl.BlockSpec((k, tn), lambda i, j: (0, j))],
        out_specs=pl.BlockSpec((tm, tn), lambda i, j: (i, j)),
        out_shape=jax.ShapeDtypeStruct((s, n), out_dtype),
        compiler_params=_params(("arbitrary", "arbitrary"), 48),
        name=name,
    )(h, w)


def _proj_gelu(h, w, b, out_dtype, tm=1024, tn=1024, name="proj_gelu"):
    s, k = h.shape
    n = w.shape[1]
    return pl.pallas_call(
        _proj_gelu_kernel,
        grid=(s // tm, n // tn),
        in_specs=[pl.BlockSpec((tm, k), lambda i, j: (i, 0)), pl.BlockSpec((k, tn), lambda i, j: (0, j)),
                  pl.BlockSpec((1, tn), lambda i, j: (0, j))],
        out_specs=pl.BlockSpec((tm, tn), lambda i, j: (i, j)),
        out_shape=jax.ShapeDtypeStruct((s, n), out_dtype),
        compiler_params=_params(("arbitrary", "arbitrary"), 48),
        name=name,
    )(h, w, _row(b))


def _proj_glu(h, w, b, tm=1024, tn=512, name="proj_glu"):
    s, k = h.shape
    n = w.shape[1] // 2
    nb = n // tn
    b2 = _row(b)
    return pl.pallas_call(
        _proj_glu_kernel,
        grid=(s // tm, nb),
        in_specs=[pl.BlockSpec((tm, k), lambda i, j: (i, 0)),
                  pl.BlockSpec((k, tn), lambda i, j: (0, j)),
                  pl.BlockSpec((k, tn), lambda i, j: (0, j + nb)),
                  pl.BlockSpec((1, tn), lambda i, j: (0, j)),
                  pl.BlockSpec((1, tn), lambda i, j: (0, j + nb))],
        out_specs=pl.BlockSpec((tm, tn), lambda i, j: (i, j)),
        out_shape=jax.ShapeDtypeStruct((s, n), F32),
        compiler_params=_params(("arbitrary", "arbitrary"), 48),
        name=name,
    )(h, w, w, b2, b2)


def _out_proj_kernel(a_ref, w_ref, b_ref, x_ref, gp_ref, gn_ref, xo_ref, ho_ref, *, n_split):
    rows = a_ref.shape[0] // n_split
    for r0 in range(0, a_ref.shape[0], rows):
        rs = slice(r0, r0 + rows)
        y = _dot(a_ref[rs, :], w_ref[...]) + b_ref[...]
        xn, hn = _residual_norm(y, x_ref[rs, :], gp_ref[...], gn_ref[...])
        xo_ref[rs, :] = xn
        ho_ref[rs, :] = hn


def _out_proj(a, w, b, x, g_post, g_next, tm=512, name="out_proj"):
    s, k = a.shape
    d = w.shape[1]
    row = pl.BlockSpec((1, d), lambda i: (0, 0))
    return pl.pallas_call(
        functools.partial(_out_proj_kernel, n_split=4),
        grid=(s // tm,),
        in_specs=[pl.BlockSpec((tm, k), lambda i: (i, 0)), pl.BlockSpec((k, d), lambda i: (0, 0)), row,
                  pl.BlockSpec((tm, d), lambda i: (i, 0)), row, row],
        out_specs=[pl.BlockSpec((tm, d), lambda i: (i, 0)), pl.BlockSpec((tm, d), lambda i: (i, 0))],
        out_shape=[jax.ShapeDtypeStruct((s, d), F32), jax.ShapeDtypeStruct((s, d), BF16)],
        compiler_params=_params(("arbitrary",), 56),
        name=name,
    )(a, w, _row(b), x, _row(g_post), _row(g_next))


def _t5_bucket(dist):
    n = np.maximum(dist, 0)
    large = REL_MAX_EXACT + (np.log(np.maximum(n, 1) / REL_MAX_EXACT)
                             / math.log(REL_MAX_DIST / REL_MAX_EXACT)
                             * (REL_BUCKETS - REL_MAX_EXACT)).astype(np.int32)
    large = np.minimum(large, REL_BUCKETS - 1)
    return np.where(n < REL_MAX_EXACT, n, large).astype(np.int32)


def _swa_geometry():
    qi = np.arange(A_BLOCK)[:, None]
    kj = np.arange(2 * A_BLOCK)[None, :]
    dist = qi + A_BLOCK - kj
    in_window = ((dist >= 0) & (dist < A_BLOCK)).astype(np.float32)
    return _t5_bucket(dist), np.tile(in_window, (A_HEADS // A_KV_HEADS // 2, 2))


def _swa_bias_kernel(tab_ref, bucket_ref, o_ref):
    pair = pl.program_id(0)
    bucket = bucket_ref[...]
    for half in range(2):
        head = 2 * pair + half
        acc = jnp.zeros(bucket.shape, F32)
        for b in range(REL_BUCKETS):
            acc = jnp.where(bucket == b, tab_ref[b, head], acc)
        o_ref[0, :, half * 2 * A_BLOCK:(half + 1) * 2 * A_BLOCK] = acc


def _swa_bias(rel_table, bucket):
    return pl.pallas_call(
        _swa_bias_kernel,
        grid=(A_HEADS // 2,),
        in_specs=[pl.BlockSpec(memory_space=pltpu.SMEM),
                  pl.BlockSpec((A_BLOCK, 2 * A_BLOCK), lambda p: (0, 0))],
        out_specs=pl.BlockSpec((1, A_BLOCK, 4 * A_BLOCK), lambda p: (p, 0, 0)),
        out_shape=jax.ShapeDtypeStruct((A_HEADS // 2, A_BLOCK, 4 * A_BLOCK), F32),
        compiler_params=_params(("arbitrary",), 16),
        name="swa_bias",
    )(rel_table.astype(F32), jnp.asarray(bucket))


def _swa_kernel(sink_ref, q_ref, kvp_ref, kvc_ref, bias_ref, win_ref, o_ref):
    n = pl.program_id(0)
    hd = A_HEAD_DIM
    nk = 2 * A_BLOCK
    npair = A_HEADS // A_KV_HEADS // 2
    rows = npair * A_BLOCK
    first_k = lax.broadcasted_iota(jnp.int32, (nk, 2 * hd), 1) < hd
    col = lax.broadcasted_iota(jnp.int32, (rows, 2 * nk), 1)
    key_valid = (n > 0) | ((col & (nk - 1)) >= A_BLOCK)
    mask = (win_ref[...] > 0.0) & key_valid
    first_o = lax.broadcasted_iota(jnp.int32, (rows, 2 * hd), 1) < hd
    row_pair = lax.broadcasted_iota(jnp.int32, (rows, 1), 0) // A_BLOCK
    kv_cols = A_KV_HEADS * 2 * hd
    for kvh in range(A_KV_HEADS):
        ks = slice(kvh * 2 * hd, (kvh + 1) * 2 * hd)
        vs = slice(kv_cols + kvh * 2 * hd, kv_cols + (kvh + 1) * 2 * hd)
        kk = jnp.concatenate([kvp_ref[:, ks], kvc_ref[:, ks]], axis=0)
        vv = jnp.concatenate([kvp_ref[:, vs], kvc_ref[:, vs]], axis=0)
        zero = jnp.zeros_like(kk)
        k_diag = jnp.concatenate([jnp.where(first_k, kk, zero), jnp.where(first_k, zero, kk)], axis=0)
        v_diag = jnp.concatenate([jnp.where(first_k, vv, zero), jnp.where(first_k, zero, vv)], axis=0)
        p0 = kvh * npair
        pair_cols = [slice((p0 + j) * 2 * hd, (p0 + j + 1) * 2 * hd) for j in range(npair)]
        q4 = jnp.concatenate([q_ref[:, c] for c in pair_cols], axis=0) * (hd ** -0.5)
        logits = _dot_nt(q4, k_diag)
        bias = bias_ref[p0:p0 + npair].reshape(rows, 2 * nk)
        logits = jnp.where(mask, logits + bias, NEG_INF)
        es, invs = [], []
        for half in range(2):
            lg = logits[:, half * nk:(half + 1) * nk]
            sink = jnp.full((rows, 1), sink_ref[2 * p0 + half], F32)
            for j in range(1, npair):
                sink = jnp.where(row_pair >= j, sink_ref[2 * (p0 + j) + half], sink)
            m = jnp.maximum(jnp.max(lg, axis=-1, keepdims=True), sink)
            e = jnp.exp(lg - m)
            den = jnp.sum(e, axis=-1, keepdims=True) + jnp.exp(sink - m)
            es.append(e.astype(BF16))
            invs.append(1.0 / den)
        o4 = _dot(jnp.concatenate(es, axis=1), v_diag)
        o4 = (o4 * jnp.where(first_o, invs[0], invs[1])).astype(o_ref.dtype)
        for j in range(npair):
            o_ref[:, pair_cols[j]] = o4[j * A_BLOCK:(j + 1) * A_BLOCK, :]


def _swa_attention(qkv, sinks, bias, window):
    s = qkv.shape[0]
    nq = A_HEADS * A_HEAD_DIM
    nkv = 2 * A_KV_HEADS * 2 * A_HEAD_DIM
    kv_blk = nq // nkv
    return pl.pallas_call(
        _swa_kernel,
        grid=(s // A_BLOCK,),
        in_specs=[pl.BlockSpec(memory_space=pltpu.SMEM),
                  pl.BlockSpec((A_BLOCK, nq), lambda n: (n, 0)),
                  pl.BlockSpec((A_BLOCK, nkv), lambda n: (jnp.maximum(n - 1, 0), kv_blk)),
                  pl.BlockSpec((A_BLOCK, nkv), lambda n: (n, kv_blk)),
                  pl.BlockSpec((A_HEADS // 2, A_BLOCK, 4 * A_BLOCK), lambda n: (0, 0, 0)),
                  pl.BlockSpec(window.shape, lambda n: (0, 0))],
        out_specs=pl.BlockSpec((A_BLOCK, nq), lambda n: (n, 0)),
        out_shape=jax.ShapeDtypeStruct((s, nq), BF16),
        compiler_params=_params(("arbitrary",), 32),
        name="swa_attention",
    )(sinks.astype(F32), qkv, qkv, qkv, bias, window)


def _swa_layer(h, x, w_qkv, sinks, w_o, rel_table, g_post, g_next):
    nq = A_HEADS * A_HEAD_DIM
    nkv = A_KV_HEADS * A_HEAD_DIM
    wq = w_qkv[:, :nq]
    wk = w_qkv[:, nq:nq + nkv].reshape(D_MODEL, A_KV_HEADS, 1, A_HEAD_DIM)
    wv = w_qkv[:, nq + nkv:].reshape(D_MODEL, A_KV_HEADS, 1, A_HEAD_DIM)
    dup = lambda w: jnp.broadcast_to(w, (D_MODEL, A_KV_HEADS, 2, A_HEAD_DIM)).reshape(D_MODEL, 2 * nkv)
    w_ext = jnp.concatenate([wq, dup(wk), dup(wv)], axis=1).astype(BF16)
    qkv = _proj(h, w_ext, BF16, name="swa_qkv")
    bucket, window = _swa_geometry()
    bias = _swa_bias(rel_table, bucket)
    o = _swa_attention(qkv, sinks, bias, jnp.asarray(window))
    return _out_proj(o, w_o.astype(BF16), jnp.zeros((D_MODEL,), F32), x, g_post, g_next, name="swa_out")


def _gla_gate_kernel(h_ref, w1_ref, w2_ref, b_ref, o_ref):
    t = _dot(h_ref[...], w1_ref[...]).astype(BF16)
    gk = _dot(t, w2_ref[...]) + b_ref[...]
    log_sig = -(jnp.maximum(-gk, 0.0) + jnp.log1p(jnp.exp(-jnp.abs(gk))))
    o_ref[...] = log_sig / B_GATE_TAU


def _gla_gate(h, w1, w2, b, tm=512):
    s, d = h.shape
    n = w2.shape[1]
    w1p = jnp.zeros((d, LANES), BF16).at[:, :B_GATE_RANK].set(w1.astype(BF16))
    w2p = jnp.zeros((LANES, n), BF16).at[:B_GATE_RANK, :].set(w2.astype(BF16))
    return pl.pallas_call(
        _gla_gate_kernel,
        grid=(s // tm,),
        in_specs=[pl.BlockSpec((tm, d), lambda i: (i, 0)), pl.BlockSpec((d, LANES), lambda i: (0, 0)),
                  pl.BlockSpec((LANES, n), lambda i: (0, 0)), pl.BlockSpec((1, n), lambda i: (0, 0))],
        out_specs=pl.BlockSpec((tm, n), lambda i: (i, 0)),
        out_shape=jax.ShapeDtypeStruct((s, n), F32),
        compiler_params=_params(("arbitrary",), 32),
        name="gla_gate",
    )(h, w1p, w2p, _row(b))


def _gla_kernel(qkvr_ref, la_ref, onorm_ref, o_ref, st_ref, *, n_sub):
    @pl.when(pl.program_id(0) == 0)
    def _():
        st_ref[...] = jnp.zeros_like(st_ref)

    c = B_CHUNK
    dk_all = B_HEADS * B_KEY_DIM
    dv_all = B_HEADS * B_VAL_DIM
    causal = (lax.broadcasted_iota(jnp.int32, (c, c), 0) >= lax.broadcasted_iota(jnp.int32, (c, c), 1))
    tri = causal.astype(BF16)

    def chunk(ci, carry):
        rows = pl.ds(pl.multiple_of(ci * c, c), c)
        la = la_ref[rows, :]
        la_hi = la.astype(BF16)
        la_lo = (la - la_hi.astype(F32)).astype(BF16)
        cum = _dot(tri, la_hi) + _dot(tri, la_lo)
        last = cum[c - 1:c, :]
        q = qkvr_ref[rows, 0:dk_all].astype(F32) * (B_KEY_DIM ** -0.5)
        k = qkvr_ref[rows, dk_all:2 * dk_all].astype(F32)
        q_dec = (q * jnp.exp(cum)).astype(BF16)
        k_inv = (k * jnp.exp(-cum)).astype(BF16)
        k_end = (k * jnp.exp(last - cum)).astype(BF16)
        decay = jnp.exp(last)
        for hh in range(B_HEADS):
            ks = slice(hh * B_KEY_DIM, (hh + 1) * B_KEY_DIM)
            v = qkvr_ref[rows, 2 * dk_all + hh * B_VAL_DIM:2 * dk_all + (hh + 1) * B_VAL_DIM]
            r = qkvr_ref[rows, 2 * dk_all + dv_all + hh * B_VAL_DIM:
                         2 * dk_all + dv_all + (hh + 1) * B_VAL_DIM].astype(F32)
            att = jnp.where(causal, _dot_nt(q_dec[:, ks], k_inv[:, ks]), 0.0).astype(BF16)
            st = st_ref[hh]
            o = _dot(att, v) + _dot_nt(q_dec[:, ks], st.astype(BF16))
            st_ref[hh] = st * decay[:, ks] + _dot_tn(v, k_end[:, ks])
            o = _rms(o) * onorm_ref[...]
            o_ref[rows, hh * B_VAL_DIM:(hh + 1) * B_VAL_DIM] = (o * _silu(r)).astype(o_ref.dtype)
        return carry

    lax.fori_loop(0, n_sub, chunk, 0, unroll=True)


def _gla_core(qkvr, log_a, o_norm, tb=256):
    s = qkvr.shape[0]
    dv_all = B_HEADS * B_VAL_DIM
    return pl.pallas_call(
        functools.partial(_gla_kernel, n_sub=tb // B_CHUNK),
        grid=(s // tb,),
        in_specs=[pl.BlockSpec((tb, qkvr.shape[1]), lambda i: (i, 0)),
                  pl.BlockSpec((tb, log_a.shape[1]), lambda i: (i, 0)),
                  pl.BlockSpec((1, B_VAL_DIM), lambda i: (0, 0))],
        out_specs=pl.BlockSpec((tb, dv_all), lambda i: (i, 0)),
        out_shape=jax.ShapeDtypeStruct((s, dv_all), BF16),
        scratch_shapes=[pltpu.VMEM((B_HEADS, B_VAL_DIM, B_KEY_DIM), F32)],
        compiler_params=_params(("arbitrary",), 32),
        name="gla_core",
    )(qkvr, log_a, _row(o_norm))


def _gla_layer(h, x, w_qkvr, w_g1, w_g2, g_bias, o_norm, w_o, g_post, g_next):
    qkvr = _proj(h, w_qkvr.astype(BF16), BF16, tn=2048, name="gla_qkvr")
    log_a = _gla_gate(h, w_g1, w_g2, g_bias)
    o = _gla_core(qkvr, log_a, o_norm)
    return _out_proj(o, w_o.astype(BF16), jnp.zeros((D_MODEL,), F32), x, g_post, g_next, name="gla_out")


def _dwconv_ln_kernel(u_ref, uprev_ref, w_ref, b_ref, lng_ref, lnb_ref, o_ref, ext_ref, z_ref, *, tm, rc, cw):
    i = pl.program_id(0)
    ext_ref[0:C_HALO, :] = jnp.where(i > 0, uprev_ref[...], 0.0)
    ext_ref[C_HALO:C_HALO + tm, :] = u_ref[...]
    d = u_ref.shape[1]
    first = C_HALO - (C_KERNEL - 1)
    sub = 8
    for r0 in range(0, tm, rc):
        for c0 in range(0, d, cw):
            cols = slice(c0, c0 + cw)
            acc = jnp.zeros((rc, cw), F32) + b_ref[:, cols]
            for b in range(sub):
                rows = rc if b == 0 else rc + sub
                part = None
                for s in range(first, first + C_KERNEL):
                    if s % sub != b:
                        continue
                    term = w_ref[s - first:s - first + 1, cols] * ext_ref[r0 + s - b:r0 + s - b + rows, cols]
                    part = term if part is None else part + term
                if b:
                    part = pltpu.roll(part, rows - b, axis=0)[0:rc, :]
                acc = acc + part
            z_ref[r0:r0 + rc, cols] = acc
    z = z_ref[...]
    zc = z - jnp.mean(z, axis=-1, keepdims=True)
    zn = zc * lax.rsqrt(jnp.mean(zc * zc, axis=-1, keepdims=True) + EPS) * lng_ref[...] + lnb_ref[...]
    o_ref[...] = _silu(zn).astype(o_ref.dtype)


def _dwconv_ln(u, w_dw, b_dw, ln_g, ln_b, tm=128, rc=128, cw=128):
    s, d = u.shape
    row = pl.BlockSpec((1, d), lambda i: (0, 0))
    halo_blocks = tm // C_HALO
    return pl.pallas_call(
        functools.partial(_dwconv_ln_kernel, tm=tm, rc=rc, cw=cw),
        grid=(s // tm,),
        in_specs=[pl.BlockSpec((tm, d), lambda i: (i, 0)),
                  pl.BlockSpec((C_HALO, d), lambda i: (jnp.maximum(i * halo_blocks - 1, 0), 0)),
                  pl.BlockSpec((C_KERNEL, d), lambda i: (0, 0)), row, row, row],
        out_specs=pl.BlockSpec((tm, d), lambda i: (i, 0)),
        out_shape=jax.ShapeDtypeStruct((s, d), BF16),
        scratch_shapes=[pltpu.VMEM((C_HALO + tm, d), F32), pltpu.VMEM((tm, d), F32)],
        compiler_params=_params(("arbitrary",), 32),
        name="dwconv_ln",
    )(u, u, w_dw.astype(F32), _row(b_dw), _row(ln_g), _row(ln_b))


def _conformer_layer(h, x, w_pw1, b_pw1, w_dw, b_dw, ln_g, ln_b, w_pw2, b_pw2, g_post, g_next):
    u = _proj_glu(h, w_pw1.astype(BF16), b_pw1, tn=1024, name="conf_pw1")
    z = _dwconv_ln(u, w_dw, b_dw, ln_g, ln_b)
    return _out_proj(z, w_pw2.astype(BF16), b_pw2, x, g_post, g_next, name="conf_out")


def _sgu_kernel(u_ref, v_ref, ws_ref, bs_ref, lng_ref, lnb_ref, wout_ref, x_ref, gp_ref, gn_ref,
                xo_ref, ho_ref, p_ref, *, tm):
    t = D_CHUNK
    gw = D_HALF // D_GROUPS
    tril = lax.broadcasted_iota(jnp.int32, (t, t), 0) >= lax.broadcasted_iota(jnp.int32, (t, t), 1)
    for c0 in range(0, tm, t):
        rs = slice(c0, c0 + t)
        v = v_ref[rs, :]
        vc = v - jnp.mean(v, axis=-1, keepdims=True)
        vn = (vc * lax.rsqrt(jnp.mean(vc * vc, axis=-1, keepdims=True) + EPS) * lng_ref[...]
              + lnb_ref[...]).astype(BF16)
        for g in range(D_GROUPS):
            w = jnp.where(tril, ws_ref[g], jnp.zeros((t, t), BF16))
            cols = slice(g * gw, (g + 1) * gw)
            sv = _dot(w, vn[:, cols]) + bs_ref[g]
            p_ref[rs, cols] = (u_ref[rs, cols].astype(F32) * sv).astype(BF16)
        y = _dot(p_ref[rs, :], wout_ref[...])
        xn, hn = _residual_norm(y, x_ref[rs, :], gp_ref[...], gn_ref[...])
        xo_ref[rs, :] = xn
        ho_ref[rs, :] = hn


def _sgu(u, v, w_s, b_s, ln_g, ln_b, w_out, x, g_post, g_next, tm=256):
    s, dh = u.shape
    d = w_out.shape[1]
    row = pl.BlockSpec((1, d), lambda i: (0, 0))
    rowh = pl.BlockSpec((1, dh), lambda i: (0, 0))
    return pl.pallas_call(
        functools.partial(_sgu_kernel, tm=tm),
        grid=(s // tm,),
        in_specs=[pl.BlockSpec((tm, dh), lambda i: (i, 0)), pl.BlockSpec((tm, dh), lambda i: (i, 0)),
                  pl.BlockSpec((D_GROUPS, D_CHUNK, D_CHUNK), lambda i: (0, 0, 0)),
                  pl.BlockSpec((D_GROUPS, D_CHUNK, 1), lambda i: (0, 0, 0)),
                  rowh, rowh,
                  pl.BlockSpec((dh, d), lambda i: (0, 0), pipeline_mode=pl.Buffered(1)),
                  pl.BlockSpec((tm, d), lambda i: (i, 0)), row, row],
        out_specs=[pl.BlockSpec((tm, d), lambda i: (i, 0)), pl.BlockSpec((tm, d), lambda i: (i, 0))],
        out_shape=[jax.ShapeDtypeStruct((s, d), F32), jax.ShapeDtypeStruct((s, d), BF16)],
        scratch_shapes=[pltpu.VMEM((tm, dh), BF16)],
        compiler_params=_params(("arbitrary",), 56),
        name="sgu",
    )(u, v, w_s.astype(BF16), b_s.astype(F32)[:, :, None], _row(ln_g), _row(ln_b), w_out.astype(BF16),
      x, _row(g_post), _row(g_next))


def _gmlp_layer(h, x, w_in, b_in, ln_g, ln_b, w_s, b_s, w_out, g_post, g_next):
    w_in = w_in.astype(BF16)
    u = _proj_gelu(h, w_in[:, :D_HALF], b_in[:D_HALF], BF16, tn=2048, name="gmlp_in_u")
    v = _proj_gelu(h, w_in[:, D_HALF:], b_in[D_HALF:], F32, tn=2048, name="gmlp_in_v")
    return _sgu(u, v, w_s, b_s, ln_g, ln_b, w_out, x, g_post, g_next)


def _mem_kv_kernel(mem_ref, g_ref, w_ref, o_ref):
    mem_n = (_rms(mem_ref[...]) * g_ref[...]).astype(BF16)
    o_ref[...] = _dot(mem_n, w_ref[...]).astype(o_ref.dtype)


def _mem_kv(mem, g, w_kv):
    m, d = mem.shape
    n = w_kv.shape[1]
    tn = n // 2
    return pl.pallas_call(
        _mem_kv_kernel,
        grid=(n // tn,),
        in_specs=[pl.BlockSpec((m, d), lambda j: (0, 0)), pl.BlockSpec((1, d), lambda j: (0, 0)),
                  pl.BlockSpec((d, tn), lambda j: (0, j))],
        out_specs=pl.BlockSpec((m, tn), lambda j: (0, j)),
        out_shape=jax.ShapeDtypeStruct((m, n), BF16),
        compiler_params=_params(("arbitrary",), 32),
        name="mem_kv",
    )(mem, _row(g), w_kv.astype(BF16))


def _xattn_kernel(h_ref, wq_ref, kv_ref, wo_ref, x_ref, gp_ref, gn_ref, xo_ref, ho_ref, *, n_split):
    hd = X_HEAD_DIM
    nq = X_HEADS * hd
    rows = h_ref.shape[0] // n_split
    for r0 in range(0, h_ref.shape[0], rows):
        rs = slice(r0, r0 + rows)
        q = (_dot(h_ref[rs, :], wq_ref[...]) * (hd ** -0.5)).astype(BF16)
        outs = []
        for hh in range(X_HEADS):
            cols = slice(hh * hd, (hh + 1) * hd)
            logits = _dot_nt(q[:, cols], kv_ref[:, cols])
            e = jnp.exp(logits - jnp.max(logits, axis=-1, keepdims=True))
            inv = 1.0 / jnp.sum(e, axis=-1, keepdims=True)
            o = _dot(e.astype(BF16), kv_ref[:, nq + hh * hd:nq + (hh + 1) * hd])
            outs.append((o * inv).astype(BF16))
        y = _dot(jnp.concatenate(outs, axis=1), wo_ref[...])
        xn, hn = _residual_norm(y, x_ref[rs, :], gp_ref[...], gn_ref[...])
        xo_ref[rs, :] = xn
        ho_ref[rs, :] = hn


def _xattn(h, x, w_q, kv, w_o, g_post, g_next, tm=512):
    s, d = h.shape
    nq = w_q.shape[1]
    row = pl.BlockSpec((1, d), lambda i: (0, 0))
    return pl.pallas_call(
        functools.partial(_xattn_kernel, n_split=1),
        grid=(s // tm,),
        in_specs=[pl.BlockSpec((tm, d), lambda i: (i, 0)), pl.BlockSpec((d, nq), lambda i: (0, 0)),
                  pl.BlockSpec(kv.shape, lambda i: (0, 0)), pl.BlockSpec((nq, d), lambda i: (0, 0)),
                  pl.BlockSpec((tm, d), lambda i: (i, 0)), row, row],
        out_specs=[pl.BlockSpec((tm, d), lambda i: (i, 0)), pl.BlockSpec((tm, d), lambda i: (i, 0))],
        out_shape=[jax.ShapeDtypeStruct((s, d), F32), jax.ShapeDtypeStruct((s, d), BF16)],
        compiler_params=_params(("arbitrary",), 48),
        name="xattn",
    )(h, w_q.astype(BF16), kv, w_o.astype(BF16), x, _row(g_post), _row(g_next))


def _ffn_kernel(h_ref, wg_ref, wu_ref, wc_ref, bc_ref, wd_ref, x_ref, gp_ref, *rest, tm, emit_h):
    if emit_h:
        gn_ref, xo_ref, ho_ref, halo_ref, ext_ref, xs_ref = rest
    else:
        xo_ref, halo_ref, ext_ref, xs_ref = rest
    i = pl.program_id(0)
    f = pl.program_id(1)
    nf = xs_ref.shape[0]
    xs_ref[f] = x_ref[...]

    @pl.when(i == 0)
    def _():
        halo_ref[f] = jnp.zeros(halo_ref.shape[1:], F32)

    @pl.when(f == 0)
    def _():
        xo_ref[...] = jnp.zeros_like(xo_ref)

    h = h_ref[...]
    gate = _dot(h, wg_ref[...])
    up = _dot(h, wu_ref[...])
    ext_ref[0:FFN_HALO, :] = halo_ref[f]
    ext_ref[FFN_HALO:FFN_HALO + tm, :] = gate
    halo_ref[f] = gate[tm - FFN_HALO:tm, :]
    conv = (wc_ref[2:3, :] * gate
            + wc_ref[1:2, :] * ext_ref[FFN_HALO - 1:FFN_HALO - 1 + tm, :]
            + wc_ref[0:1, :] * ext_ref[FFN_HALO - 2:FFN_HALO - 2 + tm, :]
            + bc_ref[...])
    act = (_gelu_tanh(conv) * up).astype(BF16)
    xo_ref[...] += _dot(act, wd_ref[...])

    @pl.when(f == pl.num_programs(1) - 1)
    def _():
        x = jnp.concatenate([xs_ref[j] for j in range(nf)], axis=1)
        xn, hn = _residual_norm(xo_ref[...], x, gp_ref[...], gn_ref[...] if emit_h else None)
        xo_ref[...] = xn
        if emit_h:
            ho_ref[...] = hn


def _ffn(h, x, layer, w_gate_up, w_conv, b_conv, w_down, g_post, g_next, tm=1024, fc=FFN_CHUNK):
    s, d = h.shape
    ff = w_down.shape[1]
    nf = ff // fc
    emit_h = g_next is not None
    row = pl.BlockSpec((1, d), lambda i, f: (0, 0))
    tile = pl.BlockSpec((tm, d), lambda i, f: (i, 0))
    x_tile = pl.BlockSpec((tm, d // nf), lambda i, f: (i, f))
    in_specs = [tile,
                pl.BlockSpec((None, d, fc), lambda i, f: (layer, 0, f)),
                pl.BlockSpec((None, d, fc), lambda i, f: (layer, 0, f + nf)),
                pl.BlockSpec((FFN_KERNEL, fc), lambda i, f: (0, f)),
                pl.BlockSpec((1, fc), lambda i, f: (0, f)),
                pl.BlockSpec((None, fc, d), lambda i, f: (layer, f, 0)),
                x_tile, row]
    args = [h, w_gate_up, w_gate_up, w_conv.astype(F32), _row(b_conv), w_down, x, _row(g_post)]
    out_specs = [tile]
    out_shape = [jax.ShapeDtypeStruct((s, d), F32)]
    if emit_h:
        in_specs.append(row)
        args.append(_row(g_next))
        out_specs.append(tile)
        out_shape.append(jax.ShapeDtypeStruct((s, d), BF16))
    outs = pl.pallas_call(
        functools.partial(_ffn_kernel, tm=tm, emit_h=emit_h),
        grid=(s // tm, nf),
        in_specs=in_specs,
        out_specs=out_specs,
        out_shape=out_shape,
        scratch_shapes=[pltpu.VMEM((nf, FFN_HALO, fc), F32), pltpu.VMEM((FFN_HALO + tm, fc), F32),
                        pltpu.VMEM((nf, tm, d // nf), F32)],
        compiler_params=_params(("arbitrary", "arbitrary"), 60),
        name="ffn",
    )(*args)
    return (outs[0], outs[1]) if emit_h else (outs[0], None)


def kernel(x, mem, norm_mix_pre, norm_mix_post, norm_mem, norm_xattn_pre, norm_xattn_post, norm_ffn_pre, norm_ffn_post, rel_bias_table, a_w_qkv, a_sinks, a_w_o, b_w_qkvr, b_w_gate1, b_w_gate2, b_gate_bias, b_o_norm, b_w_o, c_w_pw1, c_b_pw1, c_w_dw, c_b_dw, c_ln_g, c_ln_b, c_w_pw2, c_b_pw2, d_w_in, d_b_in, d_ln_g, d_ln_b, d_w_s, d_b_s, d_w_out, x_w_q, x_w_kv, x_w_o, f_w_gate_up, f_w_conv, f_b_conv, f_w_down):
    assert x.shape[0] == 1 and mem.shape[0] == 1
    xs = x[0]
    mem2 = mem[0]
    w_gate_up = f_w_gate_up.astype(BF16)
    w_down = f_w_down.astype(BF16)
    h = _rmsnorm(xs, norm_mix_pre[0])
    for i in range(DEPTH):
        kind, j = i % 4, i // 4
        g_post, g_next = norm_mix_post[i], norm_xattn_pre[i]
        if kind == 0:
            xs, h = _swa_layer(h, xs, a_w_qkv[j], a_sinks[j], a_w_o[j], rel_bias_table, g_post, g_next)
        elif kind == 1:
            xs, h = _gla_layer(h, xs, b_w_qkvr[j], b_w_gate1[j], b_w_gate2[j], b_gate_bias[j], b_o_norm[j],
                               b_w_o[j], g_post, g_next)
        elif kind == 2:
            xs, h = _conformer_layer(h, xs, c_w_pw1[j], c_b_pw1[j], c_w_dw[j], c_b_dw[j], c_ln_g[j],
                                     c_ln_b[j], c_w_pw2[j], c_b_pw2[j], g_post, g_next)
        else:
            xs, h = _gmlp_layer(h, xs, d_w_in[j], d_b_in[j], d_ln_g[j], d_ln_b[j], d_w_s[j], d_b_s[j],
                                d_w_out[j], g_post, g_next)
        kv = _mem_kv(mem2, norm_mem[i], x_w_kv[i])
        xs, h = _xattn(h, xs, x_w_q[i], kv, x_w_o[i], norm_xattn_post[i], norm_ffn_pre[i])
        g_next = norm_mix_pre[i + 1] if i + 1 < DEPTH else None
        xs, h = _ffn(h, xs, i, w_gate_up, f_w_conv[i], f_b_conv[i], w_down, norm_ffn_post[i], g_next)
    return xs[None]
```

```python
import functools
import math

import numpy as np
import jax
import jax.numpy as jnp
from jax import lax
from jax.experimental import pallas as pl
from jax.experimental.pallas import tpu as pltpu

D_MODEL = 2048
DEPTH = 4
EPS = 1e-6
NEG_INF = -1e30

A_HEADS = 32
A_KV_HEADS = 4
A_HEAD_DIM = 64
A_BLOCK = 128
REL_BUCKETS = 32
REL_MAX_EXACT = 16
REL_MAX_DIST = 128

B_HEADS = 4
B_KEY_DIM = 256
B_VAL_DIM = 512
B_GATE_RANK = 16
B_GATE_TAU = 16.0
B_CHUNK = 64

C_KERNEL = 31
C_HALO = 32

D_CHUNK = 128
D_GROUPS = 8
D_HALF = 2 * D_MODEL

X_HEADS = 4
X_HEAD_DIM = 128

FFN_DIM = 4 * D_MODEL
FFN_KERNEL = 3
FFN_HALO = 8
FFN_CHUNK = 512

LANES = 128
MIB = 1024 * 1024

BF16 = jnp.bfloat16
F32 = jnp.float32


def _params(semantics, vmem_mib):
    return pltpu.CompilerParams(dimension_semantics=semantics, vmem_limit_bytes=vmem_mib * MIB)


def _dot(a, b):
    return jnp.dot(a, b, preferred_element_type=F32)


def _dot_nt(a, b):
    return lax.dot_general(a, b, (((1,), (1,)), ((), ())), preferred_element_type=F32)


def _dot_tn(a, b):
    return lax.dot_general(a, b, (((0,), (0,)), ((), ())), preferred_element_type=F32)


def _rms(y):
    return y * lax.rsqrt(jnp.mean(y * y, axis=-1, keepdims=True) + EPS)


def _residual_norm(y, x, g_post, g_next):
    xn = x + _rms(y) * g_post
    if g_next is None:
        return xn, None
    return xn, (_rms(xn) * g_next).astype(BF16)


def _gelu_tanh(x):
    return 0.5 * x * (1.0 + jnp.tanh(math.sqrt(2.0 / math.pi) * (x + 0.044715 * (x * x * x))))


def _gelu_erf(x):
    return 0.5 * x * (1.0 + lax.erf(x * math.sqrt(0.5)))


def _silu(x):
    return x * jax.nn.sigmoid(x)


def _row(v):
    return v.reshape(1, -1).astype(F32)


def _proj_plain_kernel(h_ref, w_ref, o_ref):
    o_ref[...] = _dot(h_ref[...], w_ref[...]).astype(o_ref.dtype)


def _proj_gelu_kernel(h_ref, w_ref, b_ref, o_ref):
    o_ref[...] = _gelu_erf(_dot(h_ref[...], w_ref[...]) + b_ref[...]).astype(o_ref.dtype)


def _proj_glu_kernel(h_ref, wa_ref, wg_ref, ba_ref, bg_ref, o_ref):
    h = h_ref[...]
    a = _dot(h, wa_ref[...]) + ba_ref[...]
    g = _dot(h, wg_ref[...]) + bg_ref[...]
    o_ref[...] = (a * jax.nn.sigmoid(g)).astype(o_ref.dtype)


def _proj(h, w, out_dtype, tm=1024, tn=1024, name="proj"):
    s, k = h.shape
    n = w.shape[1]
    tn = min(tn, n)
    return pl.pallas_call(
        _proj_plain_kernel,
        grid=(s // tm, n // tn),
        in_specs=[pl.BlockSpec((tm, k), lambda i, j: (i, 0)), pl.BlockSpec((k, tn), lambda i, j: (0, j))],
        out_specs=pl.BlockSpec((tm, tn), lambda i, j: (i, j)),
        out_shape=jax.ShapeDtypeStruct((s, n), out_dtype),
        compiler_params=_params(("arbitrary", "arbitrary"), 48),
        name=name,
    )(h, w)


def _norm_proj_kernel(x_ref, g_ref, w_ref, o_ref, h_ref):
    @pl.when(pl.program_id(1) == 0)
    def _():
        h_ref[...] = (_rms(x_ref[...]) * g_ref[...]).astype(h_ref.dtype)

    o_ref[...] = _dot(h_ref[...], w_ref[...]).astype(o_ref.dtype)


def _norm_proj(x, g, w, tm=1024, name="norm_proj"):
    s, k = x.shape
    n = w.shape[1]
    tn = n // 2
    return pl.pallas_call(
        _norm_proj_kernel,
        grid=(s // tm, n // tn),
        in_specs=[pl.BlockSpec((tm, k), lambda i, j: (i, 0)), pl.BlockSpec((1, k), lambda i, j: (0, 0)),
                  pl.BlockSpec((k, tn), lambda i, j: (0, j))],
        out_specs=pl.BlockSpec((tm, tn), lambda i, j: (i, j)),
        out_shape=jax.ShapeDtypeStruct((s, n), BF16),
        scratch_shapes=[pltpu.VMEM((tm, k), BF16)],
        compiler_params=_params(("arbitrary", "arbitrary"), 48),
        name=name,
    )(x, _row(g), w)


def _proj_gelu(h, w, b, out_dtype, tm=1024, tn=1024, name="proj_gelu"):
    s, k = h.shape
    n = w.shape[1]
    return pl.pallas_call(
        _proj_gelu_kernel,
        grid=(s // tm, n // tn),
        in_specs=[pl.BlockSpec((tm, k), lambda i, j: (i, 0)), pl.BlockSpec((k, tn), lambda i, j: (0, j)),
                  pl.BlockSpec((1, tn), lambda i, j: (0, j))],
        out_specs=pl.BlockSpec((tm, tn), lambda i, j: (i, j)),
        out_shape=jax.ShapeDtypeStruct((s, n), out_dtype),
        compiler_params=_params(("arbitrary", "arbitrary"), 48),
        name=name,
    )(h, w, _row(b))


def _proj_glu(h, w, b, tm=1024, tn=512, name="proj_glu"):
    s, k = h.shape
    n = w.shape[1] // 2
    nb = n // tn
    b2 = _row(b)
    return pl.pallas_call(
        _proj_glu_kernel,
        grid=(s // tm, nb),
        in_specs=[pl.BlockSpec((tm, k), lambda i, j: (i, 0)),
                  pl.BlockSpec((k, tn), lambda i, j: (0, j)),
                  pl.BlockSpec((k, tn), lambda i, j: (0, j + nb)),
                  pl.BlockSpec((1, tn), lambda i, j: (0, j)),
                  pl.BlockSpec((1, tn), lambda i, j: (0, j + nb))],
        out_specs=pl.BlockSpec((tm, tn), lambda i, j: (i, j)),
        out_shape=jax.ShapeDtypeStruct((s, n), F32),
        compiler_params=_params(("arbitrary", "arbitrary"), 48),
        name=name,
    )(h, w, w, b2, b2)


def _out_proj_kernel(a_ref, w_ref, b_ref, x_ref, gp_ref, gn_ref, xo_ref, ho_ref, *, n_split):
    rows = a_ref.shape[0] // n_split
    for r0 in range(0, a_ref.shape[0], rows):
        rs = slice(r0, r0 + rows)
        y = _dot(a_ref[rs, :], w_ref[...]) + b_ref[...]
        xn, hn = _residual_norm(y, x_ref[rs, :], gp_ref[...], gn_ref[...])
        xo_ref[rs, :] = xn
        ho_ref[rs, :] = hn


def _out_proj(a, w, b, x, g_post, g_next, tm=512, name="out_proj"):
    s, k = a.shape
    d = w.shape[1]
    row = pl.BlockSpec((1, d), lambda i: (0, 0))
    return pl.pallas_call(
        functools.partial(_out_proj_kernel, n_split=4),
        grid=(s // tm,),
        in_specs=[pl.BlockSpec((tm, k), lambda i: (i, 0)), pl.BlockSpec((k, d), lambda i: (0, 0)), row,
                  pl.BlockSpec((tm, d), lambda i: (i, 0)), row, row],
        out_specs=[pl.BlockSpec((tm, d), lambda i: (i, 0)), pl.BlockSpec((tm, d), lambda i: (i, 0))],
        out_shape=[jax.ShapeDtypeStruct((s, d), F32), jax.ShapeDtypeStruct((s, d), BF16)],
        compiler_params=_params(("arbitrary",), 56),
        name=name,
    )(a, w, _row(b), x, _row(g_post), _row(g_next))


def _t5_bucket(dist):
    n = np.maximum(dist, 0)
    large = REL_MAX_EXACT + (np.log(np.maximum(n, 1) / REL_MAX_EXACT)
                             / math.log(REL_MAX_DIST / REL_MAX_EXACT)
                             * (REL_BUCKETS - REL_MAX_EXACT)).astype(np.int32)
    large = np.minimum(large, REL_BUCKETS - 1)
    return np.where(n < REL_MAX_EXACT, n, large).astype(np.int32)


def _swa_geometry():
    qi = np.arange(A_BLOCK)[:, None]
    kj = np.arange(2 * A_BLOCK)[None, :]
    dist = qi + A_BLOCK - kj
    in_window = ((dist >= 0) & (dist < A_BLOCK)).astype(np.float32)
    return _t5_bucket(dist), np.tile(in_window, (A_HEADS // A_KV_HEADS // 2, 2))


def _swa_bias_kernel(tab_ref, bucket_ref, o_ref):
    pair = pl.program_id(0)
    bucket = bucket_ref[...]
    for half in range(2):
        head = 2 * pair + half
        acc = jnp.zeros(bucket.shape, F32)
        for b in range(REL_BUCKETS):
            acc = jnp.where(bucket == b, tab_ref[b, head], acc)
        o_ref[0, :, half * 2 * A_BLOCK:(half + 1) * 2 * A_BLOCK] = acc


def _swa_bias(rel_table, bucket):
    return pl.pallas_call(
        _swa_bias_kernel,
        grid=(A_HEADS // 2,),
        in_specs=[pl.BlockSpec(memory_space=pltpu.SMEM),
                  pl.BlockSpec((A_BLOCK, 2 * A_BLOCK), lambda p: (0, 0))],
        out_specs=pl.BlockSpec((1, A_BLOCK, 4 * A_BLOCK), lambda p: (p, 0, 0)),
        out_shape=jax.ShapeDtypeStruct((A_HEADS // 2, A_BLOCK, 4 * A_BLOCK), F32),
        compiler_params=_params(("arbitrary",), 16),
        name="swa_bias",
    )(rel_table.astype(F32), jnp.asarray(bucket))


def _swa_kernel(sink_ref, q_ref, kvp_ref, kvc_ref, bias_ref, win_ref, o_ref):
    n = pl.program_id(0)
    hd = A_HEAD_DIM
    nk = 2 * A_BLOCK
    npair = A_HEADS // A_KV_HEADS // 2
    rows = npair * A_BLOCK
    first_k = lax.broadcasted_iota(jnp.int32, (nk, 2 * hd), 1) < hd
    col = lax.broadcasted_iota(jnp.int32, (rows, 2 * nk), 1)
    key_valid = (n > 0) | ((col & (nk - 1)) >= A_BLOCK)
    mask = (win_ref[...] > 0.0) & key_valid
    first_o = lax.broadcasted_iota(jnp.int32, (rows, 2 * hd), 1) < hd
    row_pair = lax.broadcasted_iota(jnp.int32, (rows, 1), 0) // A_BLOCK
    kv_cols = A_KV_HEADS * hd

    def lane_tile(c0):
        t = jnp.concatenate([kvp_ref[:, c0:c0 + 2 * hd], kvc_ref[:, c0:c0 + 2 * hd]], axis=0)
        swapped = pltpu.roll(t, hd, axis=1)
        return t, swapped

    def block_diag(own, other, odd):
        zero = jnp.zeros_like(own)
        top = jnp.where(first_k, other if odd else own, zero)
        bottom = jnp.where(first_k, zero, own if odd else other)
        return jnp.concatenate([top, bottom], axis=0)

    k_tiles = [lane_tile(c0) for c0 in range(0, kv_cols, 2 * hd)]
    v_tiles = [lane_tile(kv_cols + c0) for c0 in range(0, kv_cols, 2 * hd)]
    for kvh in range(A_KV_HEADS):
        k_diag = block_diag(*k_tiles[kvh // 2], kvh % 2)
        v_diag = block_diag(*v_tiles[kvh // 2], kvh % 2)
        p0 = kvh * npair
        pair_cols = [slice((p0 + j) * 2 * hd, (p0 + j + 1) * 2 * hd) for j in range(npair)]
        q4 = jnp.concatenate([q_ref[:, c] for c in pair_cols], axis=0) * (hd ** -0.5)
        logits = _dot_nt(q4, k_diag)
        bias = bias_ref[p0:p0 + npair].reshape(rows, 2 * nk)
        logits = jnp.where(mask, logits + bias, NEG_INF)
        es, invs = [], []
        for half in range(2):
            lg = logits[:, half * nk:(half + 1) * nk]
            sink = jnp.full((rows, 1), sink_ref[2 * p0 + half], F32)
            for j in range(1, npair):
                sink = jnp.where(row_pair >= j, sink_ref[2 * (p0 + j) + half], sink)
            m = jnp.maximum(jnp.max(lg, axis=-1, keepdims=True), sink)
            e = jnp.exp(lg - m)
            den = jnp.sum(e, axis=-1, keepdims=True) + jnp.exp(sink - m)
            es.append(e.astype(BF16))
            invs.append(1.0 / den)
        o4 = _dot(jnp.concatenate(es, axis=1), v_diag)
        o4 = (o4 * jnp.where(first_o, invs[0], invs[1])).astype(o_ref.dtype)
        for j in range(npair):
            o_ref[:, pair_cols[j]] = o4[j * A_BLOCK:(j + 1) * A_BLOCK, :]


def _swa_attention(qkv, sinks, bias, window):
    s = qkv.shape[0]
    nq = A_HEADS * A_HEAD_DIM
    nkv = 2 * A_KV_HEADS * A_HEAD_DIM
    kv_blk = nq // nkv
    return pl.pallas_call(
        _swa_kernel,
        grid=(s // A_BLOCK,),
        in_specs=[pl.BlockSpec(memory_space=pltpu.SMEM),
                  pl.BlockSpec((A_BLOCK, nq), lambda n: (n, 0)),
                  pl.BlockSpec((A_BLOCK, nkv), lambda n: (jnp.maximum(n - 1, 0), kv_blk)),
                  pl.BlockSpec((A_BLOCK, nkv), lambda n: (n, kv_blk)),
                  pl.BlockSpec((A_HEADS // 2, A_BLOCK, 4 * A_BLOCK), lambda n: (0, 0, 0)),
                  pl.BlockSpec(window.shape, lambda n: (0, 0))],
        out_specs=pl.BlockSpec((A_BLOCK, nq), lambda n: (n, 0)),
        out_shape=jax.ShapeDtypeStruct((s, nq), BF16),
        compiler_params=_params(("arbitrary",), 32),
        name="swa_attention",
    )(sinks.astype(F32), qkv, qkv, qkv, bias, window)


def _swa_layer(h, x, g_pre, w_qkv, sinks, w_o, rel_table, g_post, g_next):
    if h is None:
        qkv = _norm_proj(x, g_pre, w_qkv.astype(BF16), name="swa_qkv")
    else:
        qkv = _proj(h, w_qkv.astype(BF16), BF16, tn=w_qkv.shape[1] // 2, name="swa_qkv")
    bucket, window = _swa_geometry()
    bias = _swa_bias(rel_table, bucket)
    o = _swa_attention(qkv, sinks, bias, jnp.asarray(window))
    return _out_proj(o, w_o.astype(BF16), jnp.zeros((D_MODEL,), F32), x, g_post, g_next, name="swa_out")


def _gla_gate_kernel(h_ref, w1_ref, w2_ref, b_ref, o_ref):
    t = _dot(h_ref[...], w1_ref[...]).astype(BF16)
    gk = _dot(t, w2_ref[...]) + b_ref[...]
    log_sig = -(jnp.maximum(-gk, 0.0) + jnp.log1p(jnp.exp(-jnp.abs(gk))))
    o_ref[...] = log_sig / B_GATE_TAU


def _gla_gate(h, w1, w2, b, tm=512):
    s, d = h.shape
    n = w2.shape[1]
    w1p = jnp.zeros((d, LANES), BF16).at[:, :B_GATE_RANK].set(w1.astype(BF16))
    w2p = jnp.zeros((LANES, n), BF16).at[:B_GATE_RANK, :].set(w2.astype(BF16))
    return pl.pallas_call(
        _gla_gate_kernel,
        grid=(s // tm,),
        in_specs=[pl.BlockSpec((tm, d), lambda i: (i, 0)), pl.BlockSpec((d, LANES), lambda i: (0, 0)),
                  pl.BlockSpec((LANES, n), lambda i: (0, 0)), pl.BlockSpec((1, n), lambda i: (0, 0))],
        out_specs=pl.BlockSpec((tm, n), lambda i: (i, 0)),
        out_shape=jax.ShapeDtypeStruct((s, n), F32),
        compiler_params=_params(("arbitrary",), 32),
        name="gla_gate",
    )(h, w1p, w2p, _row(b))


def _gla_kernel(qkvr_ref, la_ref, onorm_ref, o_ref, st_ref, *, n_sub):
    @pl.when(pl.program_id(0) == 0)
    def _():
        st_ref[...] = jnp.zeros_like(st_ref)

    c = B_CHUNK
    dk_all = B_HEADS * B_KEY_DIM
    dv_all = B_HEADS * B_VAL_DIM
    causal = (lax.broadcasted_iota(jnp.int32, (c, c), 0) >= lax.broadcasted_iota(jnp.int32, (c, c), 1))
    tri = causal.astype(BF16)

    def chunk(ci, carry):
        rows = pl.ds(pl.multiple_of(ci * c, c), c)
        la = la_ref[rows, :]
        la_hi = la.astype(BF16)
        la_lo = (la - la_hi.astype(F32)).astype(BF16)
        cum = _dot(tri, la_hi) + _dot(tri, la_lo)
        last = cum[c - 1:c, :]
        q = qkvr_ref[rows, 0:dk_all].astype(F32) * (B_KEY_DIM ** -0.5)
        k = qkvr_ref[rows, dk_all:2 * dk_all].astype(F32)
        q_dec = (q * jnp.exp(cum)).astype(BF16)
        k_inv = (k * jnp.exp(-cum)).astype(BF16)
        k_end = (k * jnp.exp(last - cum)).astype(BF16)
        decay = jnp.exp(last)
        for hh in range(B_HEADS):
            ks = slice(hh * B_KEY_DIM, (hh + 1) * B_KEY_DIM)
            v = qkvr_ref[rows, 2 * dk_all + hh * B_VAL_DIM:2 * dk_all + (hh + 1) * B_VAL_DIM]
            r = qkvr_ref[rows, 2 * dk_all + dv_all + hh * B_VAL_DIM:
                         2 * dk_all + dv_all + (hh + 1) * B_VAL_DIM].astype(F32)
            att = jnp.where(causal, _dot_nt(q_dec[:, ks], k_inv[:, ks]), 0.0).astype(BF16)
            st = st_ref[hh]
            o = _dot(att, v) + _dot_nt(q_dec[:, ks], st.astype(BF16))
            st_ref[hh] = st * decay[:, ks] + _dot_tn(v, k_end[:, ks])
            o = _rms(o) * onorm_ref[...]
            o_ref[rows, hh * B_VAL_DIM:(hh + 1) * B_VAL_DIM] = (o * _silu(r)).astype(o_ref.dtype)
        return carry

    lax.fori_loop(0, n_sub, chunk, 0, unroll=True)


def _gla_core(qkvr, log_a, o_norm, tb=256):
    s = qkvr.shape[0]
    dv_all = B_HEADS * B_VAL_DIM
    return pl.pallas_call(
        functools.partial(_gla_kernel, n_sub=tb // B_CHUNK),
        grid=(s // tb,),
        in_specs=[pl.BlockSpec((tb, qkvr.shape[1]), lambda i: (i, 0)),
                  pl.BlockSpec((tb, log_a.shape[1]), lambda i: (i, 0)),
                  pl.BlockSpec((1, B_VAL_DIM), lambda i: (0, 0))],
        out_specs=pl.BlockSpec((tb, dv_all), lambda i: (i, 0)),
        out_shape=jax.ShapeDtypeStruct((s, dv_all), BF16),
        scratch_shapes=[pltpu.VMEM((B_HEADS, B_VAL_DIM, B_KEY_DIM), F32)],
        compiler_params=_params(("arbitrary",), 32),
        name="gla_core",
    )(qkvr, log_a, _row(o_norm))


def _gla_layer(h, x, w_qkvr, w_g1, w_g2, g_bias, o_norm, w_o, g_post, g_next):
    qkvr = _proj(h, w_qkvr.astype(BF16), BF16, tn=2048, name="gla_qkvr")
    log_a = _gla_gate(h, w_g1, w_g2, g_bias)
    o = _gla_core(qkvr, log_a, o_norm)
    return _out_proj(o, w_o.astype(BF16), jnp.zeros((D_MODEL,), F32), x, g_post, g_next, name="gla_out")


def _dwconv_ln_kernel(u_ref, uprev_ref, w_ref, b_ref, lng_ref, lnb_ref, o_ref, ext_ref, z_ref, *, tm, rc, cw):
    i = pl.program_id(0)
    ext_ref[0:C_HALO, :] = jnp.where(i > 0, uprev_ref[...], 0.0)
    ext_ref[C_HALO:C_HALO + tm, :] = u_ref[...]
    d = u_ref.shape[1]
    first = C_HALO - (C_KERNEL - 1)
    sub = 8
    for r0 in range(0, tm, rc):
        for c0 in range(0, d, cw):
            cols = slice(c0, c0 + cw)
            acc = jnp.zeros((rc, cw), F32) + b_ref[:, cols]
            for b in range(sub):
                rows = rc if b == 0 else rc + sub
                part = None
                for s in range(first, first + C_KERNEL):
                    if s % sub != b:
                        continue
                    term = w_ref[s - first:s - first + 1, cols] * ext_ref[r0 + s - b:r0 + s - b + rows, cols]
                    part = term if part is None else part + term
                if b:
                    part = pltpu.roll(part, rows - b, axis=0)[0:rc, :]
                acc = acc + part
            z_ref[r0:r0 + rc, cols] = acc
    z = z_ref[...]
    zc = z - jnp.mean(z, axis=-1, keepdims=True)
    zn = zc * lax.rsqrt(jnp.mean(zc * zc, axis=-1, keepdims=True) + EPS) * lng_ref[...] + lnb_ref[...]
    o_ref[...] = _silu(zn).astype(o_ref.dtype)


def _dwconv_ln(u, w_dw, b_dw, ln_g, ln_b, tm=128, rc=128, cw=128):
    s, d = u.shape
    row = pl.BlockSpec((1, d), lambda i: (0, 0))
    halo_blocks = tm // C_HALO
    return pl.pallas_call(
        functools.partial(_dwconv_ln_kernel, tm=tm, rc=rc, cw=cw),
        grid=(s // tm,),
        in_specs=[pl.BlockSpec((tm, d), lambda i: (i, 0)),
                  pl.BlockSpec((C_HALO, d), lambda i: (jnp.maximum(i * halo_blocks - 1, 0), 0)),
                  pl.BlockSpec((C_KERNEL, d), lambda i: (0, 0)), row, row, row],
        out_specs=pl.BlockSpec((tm, d), lambda i: (i, 0)),
        out_shape=jax.ShapeDtypeStruct((s, d), BF16),
        scratch_shapes=[pltpu.VMEM((C_HALO + tm, d), F32), pltpu.VMEM((tm, d), F32)],
        compiler_params=_params(("arbitrary",), 32),
        name="dwconv_ln",
    )(u, u, w_dw.astype(F32), _row(b_dw), _row(ln_g), _row(ln_b))


def _conformer_layer(h, x, w_pw1, b_pw1, w_dw, b_dw, ln_g, ln_b, w_pw2, b_pw2, g_post, g_next):
    u = _proj_glu(h, w_pw1.astype(BF16), b_pw1, tn=1024, name="conf_pw1")
    z = _dwconv_ln(u, w_dw, b_dw, ln_g, ln_b)
    return _out_proj(z, w_pw2.astype(BF16), b_pw2, x, g_post, g_next, name="conf_out")


def _sgu_kernel(u_ref, v_ref, ws_ref, bs_ref, lng_ref, lnb_ref, wout_ref, x_ref, gp_ref, gn_ref,
                xo_ref, ho_ref, p_ref, *, tm):
    t = D_CHUNK
    gw = D_HALF // D_GROUPS
    tril = lax.broadcasted_iota(jnp.int32, (t, t), 0) >= lax.broadcasted_iota(jnp.int32, (t, t), 1)
    for c0 in range(0, tm, t):
        rs = slice(c0, c0 + t)
        v = v_ref[rs, :]
        vc = v - jnp.mean(v, axis=-1, keepdims=True)
        vn = (vc * lax.rsqrt(jnp.mean(vc * vc, axis=-1, keepdims=True) + EPS) * lng_ref[...]
              + lnb_ref[...]).astype(BF16)
        for g in range(D_GROUPS):
            w = jnp.where(tril, ws_ref[g], jnp.zeros((t, t), BF16))
            cols = slice(g * gw, (g + 1) * gw)
            sv = _dot(w, vn[:, cols]) + bs_ref[g]
            p_ref[rs, cols] = (u_ref[rs, cols].astype(F32) * sv).astype(BF16)
        y = _dot(p_ref[rs, :], wout_ref[...])
        xn, hn = _residual_norm(y, x_ref[rs, :], gp_ref[...], gn_ref[...])
        xo_ref[rs, :] = xn
        ho_ref[rs, :] = hn


def _sgu(u, v, w_s, b_s, ln_g, ln_b, w_out, x, g_post, g_next, tm=256):
    s, dh = u.shape
    d = w_out.shape[1]
    row = pl.BlockSpec((1, d), lambda i: (0, 0))
    rowh = pl.BlockSpec((1, dh), lambda i: (0, 0))
    return pl.pallas_call(
        functools.partial(_sgu_kernel, tm=tm),
        grid=(s // tm,),
        in_specs=[pl.BlockSpec((tm, dh), lambda i: (i, 0)), pl.BlockSpec((tm, dh), lambda i: (i, 0)),
                  pl.BlockSpec((D_GROUPS, D_CHUNK, D_CHUNK), lambda i: (0, 0, 0)),
                  pl.BlockSpec((D_GROUPS, D_CHUNK, 1), lambda i: (0, 0, 0)),
                  rowh, rowh,
                  pl.BlockSpec((dh, d), lambda i: (0, 0), pipeline_mode=pl.Buffered(1)),
                  pl.BlockSpec((tm, d), lambda i: (i, 0)), row, row],
        out_specs=[pl.BlockSpec((tm, d), lambda i: (i, 0)), pl.BlockSpec((tm, d), lambda i: (i, 0))],
        out_shape=[jax.ShapeDtypeStruct((s, d), F32), jax.ShapeDtypeStruct((s, d), BF16)],
        scratch_shapes=[pltpu.VMEM((tm, dh), BF16)],
        compiler_params=_params(("arbitrary",), 56),
        name="sgu",
    )(u, v, w_s.astype(BF16), b_s.astype(F32)[:, :, None], _row(ln_g), _row(ln_b), w_out.astype(BF16),
      x, _row(g_post), _row(g_next))


def _gmlp_layer(h, x, w_in, b_in, ln_g, ln_b, w_s, b_s, w_out, g_post, g_next):
    w_in = w_in.astype(BF16)
    u = _proj_gelu(h, w_in[:, :D_HALF], b_in[:D_HALF], BF16, tn=2048, name="gmlp_in_u")
    v = _proj_gelu(h, w_in[:, D_HALF:], b_in[D_HALF:], F32, tn=2048, name="gmlp_in_v")
    return _sgu(u, v, w_s, b_s, ln_g, ln_b, w_out, x, g_post, g_next)


def _mem_kv_kernel(mem_ref, g_ref, w_ref, o_ref):
    mem_n = (_rms(mem_ref[...]) * g_ref[...]).astype(BF16)
    o_ref[...] = _dot(mem_n, w_ref[...]).astype(o_ref.dtype)


def _mem_kv(mem, g, w_kv):
    m, d = mem.shape
    n = w_kv.shape[1]
    tn = n // 2
    return pl.pallas_call(
        _mem_kv_kernel,
        grid=(n // tn,),
        in_specs=[pl.BlockSpec((m, d), lambda j: (0, 0)), pl.BlockSpec((1, d), lambda j: (0, 0)),
                  pl.BlockSpec((d, tn), lambda j: (0, j))],
        out_specs=pl.BlockSpec((m, tn), lambda j: (0, j)),
        out_shape=jax.ShapeDtypeStruct((m, n), BF16),
        compiler_params=_params(("arbitrary",), 32),
        name="mem_kv",
    )(mem, _row(g), w_kv.astype(BF16))


def _xattn_kernel(h_ref, wq_ref, kv_ref, wo_ref, x_ref, gp_ref, gn_ref, xo_ref, ho_ref, *, n_split):
    hd = X_HEAD_DIM
    nq = X_HEADS * hd
    rows = h_ref.shape[0] // n_split
    for r0 in range(0, h_ref.shape[0], rows):
        rs = slice(r0, r0 + rows)
        q = (_dot(h_ref[rs, :], wq_ref[...]) * (hd ** -0.5)).astype(BF16)
        outs = []
        for hh in range(X_HEADS):
            cols = slice(hh * hd, (hh + 1) * hd)
            logits = _dot_nt(q[:, cols], kv_ref[:, cols])
            e = jnp.exp(logits - jnp.max(logits, axis=-1, keepdims=True))
            inv = 1.0 / jnp.sum(e, axis=-1, keepdims=True)
            o = _dot(e.astype(BF16), kv_ref[:, nq + hh * hd:nq + (hh + 1) * hd])
            outs.append((o * inv).astype(BF16))
        y = _dot(jnp.concatenate(outs, axis=1), wo_ref[...])
        xn, hn = _residual_norm(y, x_ref[rs, :], gp_ref[...], gn_ref[...])
        xo_ref[rs, :] = xn
        ho_ref[rs, :] = hn


def _xattn(h, x, w_q, kv, w_o, g_post, g_next, tm=512):
    s, d = h.shape
    nq = w_q.shape[1]
    row = pl.BlockSpec((1, d), lambda i: (0, 0))
    return pl.pallas_call(
        functools.partial(_xattn_kernel, n_split=1),
        grid=(s // tm,),
        in_specs=[pl.BlockSpec((tm, d), lambda i: (i, 0)), pl.BlockSpec((d, nq), lambda i: (0, 0)),
                  pl.BlockSpec(kv.shape, lambda i: (0, 0)), pl.BlockSpec((nq, d), lambda i: (0, 0)),
                  pl.BlockSpec((tm, d), lambda i: (i, 0)), row, row],
        out_specs=[pl.BlockSpec((tm, d), lambda i: (i, 0)), pl.BlockSpec((tm, d), lambda i: (i, 0))],
        out_shape=[jax.ShapeDtypeStruct((s, d), F32), jax.ShapeDtypeStruct((s, d), BF16)],
        compiler_params=_params(("arbitrary",), 48),
        name="xattn",
    )(h, w_q.astype(BF16), kv, w_o.astype(BF16), x, _row(g_post), _row(g_next))


def _ffn_kernel(h_ref, wg_ref, wu_ref, wc_ref, bc_ref, wd_ref, x_ref, gp_ref, *rest, tm, emit_h):
    if emit_h:
        gn_ref, xo_ref, ho_ref, halo_ref, ext_ref, xs_ref = rest
    else:
        xo_ref, halo_ref, ext_ref, xs_ref = rest
    i = pl.program_id(0)
    f = pl.program_id(1)
    nf = xs_ref.shape[0]
    xs_ref[f] = x_ref[...]

    @pl.when(i == 0)
    def _():
        halo_ref[f] = jnp.zeros(halo_ref.shape[1:], F32)

    @pl.when(f == 0)
    def _():
        xo_ref[...] = jnp.zeros_like(xo_ref)

    h = h_ref[...]
    gate = _dot(h, wg_ref[...])
    up = _dot(h, wu_ref[...])
    ext_ref[0:FFN_HALO, :] = halo_ref[f]
    ext_ref[FFN_HALO:FFN_HALO + tm, :] = gate
    halo_ref[f] = gate[tm - FFN_HALO:tm, :]
    conv = (wc_ref[2:3, :] * gate
            + wc_ref[1:2, :] * ext_ref[FFN_HALO - 1:FFN_HALO - 1 + tm, :]
            + wc_ref[0:1, :] * ext_ref[FFN_HALO - 2:FFN_HALO - 2 + tm, :]
            + bc_ref[...])
    act = (_gelu_tanh(conv) * up).astype(BF16)
    xo_ref[...] += _dot(act, wd_ref[...])

    @pl.when(f == pl.num_programs(1) - 1)
    def _():
        x = jnp.concatenate([xs_ref[j] for j in range(nf)], axis=1)
        xn, hn = _residual_norm(xo_ref[...], x, gp_ref[...], gn_ref[...] if emit_h else None)
        xo_ref[...] = xn
        if emit_h:
            ho_ref[...] = hn


def _ffn(h, x, layer, w_gate_up, w_conv, b_conv, w_down, g_post, g_next, tm=1024, fc=FFN_CHUNK):
    s, d = h.shape
    ff = w_down.shape[1]
    nf = ff // fc
    emit_h = g_next is not None
    row = pl.BlockSpec((1, d), lambda i, f: (0, 0))
    tile = pl.BlockSpec((tm, d), lambda i, f: (i, 0))
    x_tile = pl.BlockSpec((tm, d // nf), lambda i, f: (i, f))
    in_specs = [tile,
                pl.BlockSpec((None, d, fc), lambda i, f: (layer, 0, f)),
                pl.BlockSpec((None, d, fc), lambda i, f: (layer, 0, f + nf)),
                pl.BlockSpec((FFN_KERNEL, fc), lambda i, f: (0, f)),
                pl.BlockSpec((1, fc), lambda i, f: (0, f)),
                pl.BlockSpec((None, fc, d), lambda i, f: (layer, f, 0)),
                x_tile, row]
    args = [h, w_gate_up, w_gate_up, w_conv.astype(F32), _row(b_conv), w_down, x, _row(g_post)]
    out_specs = [tile]
    out_shape = [jax.ShapeDtypeStruct((s, d), F32)]
    if emit_h:
        in_specs.append(row)
        args.append(_row(g_next))
        out_specs.append(tile)
        out_shape.append(jax.ShapeDtypeStruct((s, d), BF16))
    outs = pl.pallas_call(
        functools.partial(_ffn_kernel, tm=tm, emit_h=emit_h),
        grid=(s // tm, nf),
        in_specs=in_specs,
        out_specs=out_specs,
        out_shape=out_shape,
        scratch_shapes=[pltpu.VMEM((nf, FFN_HALO, fc), F32), pltpu.VMEM((FFN_HALO + tm, fc), F32),
                        pltpu.VMEM((nf, tm, d // nf), F32)],
        compiler_params=_params(("arbitrary", "arbitrary"), 60),
        name="ffn",
    )(*args)
    return (outs[0], outs[1]) if emit_h else (outs[0], None)


def kernel(x, mem, norm_mix_pre, norm_mix_post, norm_mem, norm_xattn_pre, norm_xattn_post, norm_ffn_pre, norm_ffn_post, rel_bias_table, a_w_qkv, a_sinks, a_w_o, b_w_qkvr, b_w_gate1, b_w_gate2, b_gate_bias, b_o_norm, b_w_o, c_w_pw1, c_b_pw1, c_w_dw, c_b_dw, c_ln_g, c_ln_b, c_w_pw2, c_b_pw2, d_w_in, d_b_in, d_ln_g, d_ln_b, d_w_s, d_b_s, d_w_out, x_w_q, x_w_kv, x_w_o, f_w_gate_up, f_w_conv, f_b_conv, f_w_down):
    assert x.shape[0] == 1 and mem.shape[0] == 1
    xs = x[0]
    mem2 = mem[0]
    w_gate_up = f_w_gate_up.astype(BF16)
    w_down = f_w_down.astype(BF16)
    h = None
    for i in range(DEPTH):
        kind, j = i % 4, i // 4
        g_post, g_next = norm_mix_post[i], norm_xattn_pre[i]
        if kind == 0:
            xs, h = _swa_layer(h, xs, norm_mix_pre[i], a_w_qkv[j], a_sinks[j], a_w_o[j], rel_bias_table,
                               g_post, g_next)
        elif kind == 1:
            xs, h = _gla_layer(h, xs, b_w_qkvr[j], b_w_gate1[j], b_w_gate2[j], b_gate_bias[j], b_o_norm[j],
                               b_w_o[j], g_post, g_next)
        elif kind == 2:
            xs, h = _conformer_layer(h, xs, c_w_pw1[j], c_b_pw1[j], c_w_dw[j], c_b_dw[j], c_ln_g[j],
                                     c_ln_b[j], c_w_pw2[j], c_b_pw2[j], g_post, g_next)
        else:
            xs, h = _gmlp_layer(h, xs, d_w_in[j], d_b_in[j], d_ln_g[j], d_ln_b[j], d_w_s[j], d_b_s[j],
                                d_w_out[j], g_post, g_next)
        kv = _mem_kv(mem2, norm_mem[i], x_w_kv[i])
        xs, h = _xattn(h, xs, x_w_q[i], kv, x_w_o[i], norm_xattn_post[i], norm_ffn_pre[i])
        g_next = norm_mix_pre[i + 1] if i + 1 < DEPTH else None
        xs, h = _ffn(h, xs, i, w_gate_up, f_w_conv[i], f_b_conv[i], w_down, norm_ffn_post[i], g_next)
    return xs[None]
```

```python
import functools
import math

import numpy as np
import jax
import jax.numpy as jnp
from jax import lax
from jax.experimental import pallas as pl
from jax.experimental.pallas import tpu as pltpu

D_MODEL = 2048
DEPTH = 4
EPS = 1e-6
NEG_INF = -1e30

A_HEADS = 32
A_KV_HEADS = 4
A_HEAD_DIM = 64
A_BLOCK = 128
REL_BUCKETS = 32
REL_MAX_EXACT = 16
REL_MAX_DIST = 128

B_HEADS = 4
B_KEY_DIM = 256
B_VAL_DIM = 512
B_GATE_RANK = 16
B_GATE_TAU = 16.0
B_CHUNK = 64

C_KERNEL = 31
C_HALO = 32

D_CHUNK = 128
D_GROUPS = 8
D_HALF = 2 * D_MODEL

X_HEADS = 4
X_HEAD_DIM = 128

FFN_DIM = 4 * D_MODEL
FFN_KERNEL = 3
FFN_HALO = 8
FFN_CHUNK = 512

LANES = 128
SUBLANES = 8
MIB = 1024 * 1024
V7X_VMEM_MIB = 64
VMEM_MIB = dict(proj=48, out_proj=56, swa_bias=16, swa=32, gla_gate=32, gla=32, dwconv=32, sgu=56, mem_kv=32,
                xattn=48, ffn=60)
assert max(VMEM_MIB.values()) < V7X_VMEM_MIB

BF16 = jnp.bfloat16
F32 = jnp.float32


def _params(semantics, vmem_mib):
    return pltpu.CompilerParams(dimension_semantics=semantics, vmem_limit_bytes=vmem_mib * MIB)


def _dot(a, b):
    return jnp.dot(a, b, preferred_element_type=F32)


def _dot_nt(a, b):
    return lax.dot_general(a, b, (((1,), (1,)), ((), ())), preferred_element_type=F32)


def _dot_tn(a, b):
    return lax.dot_general(a, b, (((0,), (0,)), ((), ())), preferred_element_type=F32)


def _rms(y):
    return y * lax.rsqrt(jnp.mean(y * y, axis=-1, keepdims=True) + EPS)


def _residual_norm(y, x, g_post, g_next):
    xn = x + _rms(y) * g_post
    if g_next is None:
        return xn, None
    return xn, (_rms(xn) * g_next).astype(BF16)


def _gelu_tanh(x):
    return 0.5 * x * (1.0 + jnp.tanh(math.sqrt(2.0 / math.pi) * (x + 0.044715 * (x * x * x))))


def _gelu_erf(x):
    return 0.5 * x * (1.0 + lax.erf(x * math.sqrt(0.5)))


def _silu(x):
    return x * jax.nn.sigmoid(x)


def _row(v):
    return v.reshape(1, -1).astype(F32)


def _proj_plain_kernel(h_ref, w_ref, o_ref):
    o_ref[...] = _dot(h_ref[...], w_ref[...]).astype(o_ref.dtype)


def _proj_gelu_kernel(h_ref, w_ref, b_ref, o_ref):
    o_ref[...] = _gelu_erf(_dot(h_ref[...], w_ref[...]) + b_ref[...]).astype(o_ref.dtype)


def _proj_glu_kernel(h_ref, wa_ref, wg_ref, ba_ref, bg_ref, o_ref):
    h = h_ref[...]
    a = _dot(h, wa_ref[...]) + ba_ref[...]
    g = _dot(h, wg_ref[...]) + bg_ref[...]
    o_ref[...] = (a * jax.nn.sigmoid(g)).astype(o_ref.dtype)


def _proj(h, w, out_dtype, tm=1024, tn=1024, name="proj"):
    s, k = h.shape
    n = w.shape[1]
    tn = min(tn, n)
    return pl.pallas_call(
        _proj_plain_kernel,
        grid=(s // tm, n // tn),
        in_specs=[pl.BlockSpec((tm, k), lambda i, j: (i, 0)), pl.BlockSpec((k, tn), lambda i, j: (0, j))],
        out_specs=pl.BlockSpec((tm, tn), lambda i, j: (i, j)),
        out_shape=jax.ShapeDtypeStruct((s, n), out_dtype),
        compiler_params=_params(("arbitrary", "arbitrary"), VMEM_MIB["proj"]),
        name=name,
    )(h, w)


def _norm_proj_kernel(x_ref, g_ref, w_ref, o_ref, h_ref):
    @pl.when(pl.program_id(1) == 0)
    def _():
        h_ref[...] = (_rms(x_ref[...]) * g_ref[...]).astype(h_ref.dtype)

    o_ref[...] = _dot(h_ref[...], w_ref[...]).astype(o_ref.dtype)


def _norm_proj(x, g, w, tm=1024, name="norm_proj"):
    s, k = x.shape
    n = w.shape[1]
    tn = n // 2
    return pl.pallas_call(
        _norm_proj_kernel,
        grid=(s // tm, n // tn),
        in_specs=[pl.BlockSpec((tm, k), lambda i, j: (i, 0)), pl.BlockSpec((1, k), lambda i, j: (0, 0)),
                  pl.BlockSpec((k, tn), lambda i, j: (0, j))],
        out_specs=pl.BlockSpec((tm, tn), lambda i, j: (i, j)),
        out_shape=jax.ShapeDtypeStruct((s, n), BF16),
        scratch_shapes=[pltpu.VMEM((tm, k), BF16)],
        compiler_params=_params(("arbitrary", "arbitrary"), VMEM_MIB["proj"]),
        name=name,
    )(x, _row(g), w)


def _proj_gelu(h, w, b, out_dtype, tm=1024, tn=1024, name="proj_gelu"):
    s, k = h.shape
    n = w.shape[1]
    return pl.pallas_call(
        _proj_gelu_kernel,
        grid=(s // tm, n // tn),
        in_specs=[pl.BlockSpec((tm, k), lambda i, j: (i, 0)), pl.BlockSpec((k, tn), lambda i, j: (0, j)),
                  pl.BlockSpec((1, tn), lambda i, j: (0, j))],
        out_specs=pl.BlockSpec((tm, tn), lambda i, j: (i, j)),
        out_shape=jax.ShapeDtypeStruct((s, n), out_dtype),
        compiler_params=_params(("arbitrary", "arbitrary"), VMEM_MIB["proj"]),
        name=name,
    )(h, w, _row(b))


def _proj_glu(h, w, b, tm=1024, tn=512, name="proj_glu"):
    s, k = h.shape
    n = w.shape[1] // 2
    nb = n // tn
    b2 = _row(b)
    return pl.pallas_call(
        _proj_glu_kernel,
        grid=(s // tm, nb),
        in_specs=[pl.BlockSpec((tm, k), lambda i, j: (i, 0)),
                  pl.BlockSpec((k, tn), lambda i, j: (0, j)),
                  pl.BlockSpec((k, tn), lambda i, j: (0, j + nb)),
                  pl.BlockSpec((1, tn), lambda i, j: (0, j)),
                  pl.BlockSpec((1, tn), lambda i, j: (0, j + nb))],
        out_specs=pl.BlockSpec((tm, tn), lambda i, j: (i, j)),
        out_shape=jax.ShapeDtypeStruct((s, n), F32),
        compiler_params=_params(("arbitrary", "arbitrary"), VMEM_MIB["proj"]),
        name=name,
    )(h, w, w, b2, b2)


def _out_proj_kernel(a_ref, w_ref, b_ref, x_ref, gp_ref, gn_ref, xo_ref, ho_ref, *, n_split):
    rows = a_ref.shape[0] // n_split
    for r0 in range(0, a_ref.shape[0], rows):
        rs = slice(r0, r0 + rows)
        y = _dot(a_ref[rs, :], w_ref[...]) + b_ref[...]
        xn, hn = _residual_norm(y, x_ref[rs, :], gp_ref[...], gn_ref[...])
        xo_ref[rs, :] = xn
        ho_ref[rs, :] = hn


def _out_proj(a, w, b, x, g_post, g_next, tm=512, name="out_proj"):
    s, k = a.shape
    d = w.shape[1]
    row = pl.BlockSpec((1, d), lambda i: (0, 0))
    return pl.pallas_call(
        functools.partial(_out_proj_kernel, n_split=4),
        grid=(s // tm,),
        in_specs=[pl.BlockSpec((tm, k), lambda i: (i, 0)), pl.BlockSpec((k, d), lambda i: (0, 0)), row,
                  pl.BlockSpec((tm, d), lambda i: (i, 0)), row, row],
        out_specs=[pl.BlockSpec((tm, d), lambda i: (i, 0)), pl.BlockSpec((tm, d), lambda i: (i, 0))],
        out_shape=[jax.ShapeDtypeStruct((s, d), F32), jax.ShapeDtypeStruct((s, d), BF16)],
        compiler_params=_params(("arbitrary",), VMEM_MIB["out_proj"]),
        name=name,
    )(a, w, _row(b), x, _row(g_post), _row(g_next))


def _t5_bucket(dist):
    n = np.maximum(dist, 0)
    large = REL_MAX_EXACT + (np.log(np.maximum(n, 1) / REL_MAX_EXACT)
                             / math.log(REL_MAX_DIST / REL_MAX_EXACT)
                             * (REL_BUCKETS - REL_MAX_EXACT)).astype(np.int32)
    large = np.minimum(large, REL_BUCKETS - 1)
    return np.where(n < REL_MAX_EXACT, n, large).astype(np.int32)


def _swa_geometry():
    qi = np.arange(A_BLOCK)[:, None]
    kj = np.arange(2 * A_BLOCK)[None, :]
    dist = qi + A_BLOCK - kj
    in_window = ((dist >= 0) & (dist < A_BLOCK)).astype(np.float32)
    return _t5_bucket(dist), np.tile(in_window, (A_HEADS // A_KV_HEADS // 2, 2))


def _swa_bias_kernel(tab_ref, bucket_ref, o_ref):
    pair = pl.program_id(0)
    bucket = bucket_ref[...]
    for half in range(2):
        head = 2 * pair + half
        acc = jnp.zeros(bucket.shape, F32)
        for b in range(REL_BUCKETS):
            acc = jnp.where(bucket == b, tab_ref[b, head], acc)
        o_ref[0, :, half * 2 * A_BLOCK:(half + 1) * 2 * A_BLOCK] = acc


def _swa_bias(rel_table, bucket):
    return pl.pallas_call(
        _swa_bias_kernel,
        grid=(A_HEADS // 2,),
        in_specs=[pl.BlockSpec(memory_space=pltpu.SMEM),
                  pl.BlockSpec((A_BLOCK, 2 * A_BLOCK), lambda p: (0, 0))],
        out_specs=pl.BlockSpec((1, A_BLOCK, 4 * A_BLOCK), lambda p: (p, 0, 0)),
        out_shape=jax.ShapeDtypeStruct((A_HEADS // 2, A_BLOCK, 4 * A_BLOCK), F32),
        compiler_params=_params(("arbitrary",), VMEM_MIB["swa_bias"]),
        name="swa_bias",
    )(rel_table.astype(F32), jnp.asarray(bucket))


def _swa_kernel(sink_ref, q_ref, kvp_ref, kvc_ref, bias_ref, win_ref, o_ref):
    n = pl.program_id(0)
    hd = A_HEAD_DIM
    nk = 2 * A_BLOCK
    npair = A_HEADS // A_KV_HEADS // 2
    rows = npair * A_BLOCK
    first_k = lax.broadcasted_iota(jnp.int32, (nk, 2 * hd), 1) < hd
    col = lax.broadcasted_iota(jnp.int32, (rows, 2 * nk), 1)
    key_valid = (n > 0) | ((col & (nk - 1)) >= A_BLOCK)
    mask = (win_ref[...] > 0.0) & key_valid
    first_o = lax.broadcasted_iota(jnp.int32, (rows, 2 * hd), 1) < hd
    row_pair = lax.broadcasted_iota(jnp.int32, (rows, 1), 0) // A_BLOCK
    kv_cols = A_KV_HEADS * hd

    def lane_tile(c0):
        t = jnp.concatenate([kvp_ref[:, c0:c0 + 2 * hd], kvc_ref[:, c0:c0 + 2 * hd]], axis=0)
        swapped = pltpu.roll(t, hd, axis=1)
        return t, swapped

    def block_diag(own, other, odd):
        zero = jnp.zeros_like(own)
        top = jnp.where(first_k, other if odd else own, zero)
        bottom = jnp.where(first_k, zero, own if odd else other)
        return jnp.concatenate([top, bottom], axis=0)

    k_tiles = [lane_tile(c0) for c0 in range(0, kv_cols, 2 * hd)]
    v_tiles = [lane_tile(kv_cols + c0) for c0 in range(0, kv_cols, 2 * hd)]
    for kvh in range(A_KV_HEADS):
        k_diag = block_diag(*k_tiles[kvh // 2], kvh % 2)
        v_diag = block_diag(*v_tiles[kvh // 2], kvh % 2)
        p0 = kvh * npair
        pair_cols = [slice((p0 + j) * 2 * hd, (p0 + j + 1) * 2 * hd) for j in range(npair)]
        q4 = jnp.concatenate([q_ref[:, c] for c in pair_cols], axis=0) * (hd ** -0.5)
        logits = _dot_nt(q4, k_diag)
        bias = bias_ref[p0:p0 + npair].reshape(rows, 2 * nk)
        logits = jnp.where(mask, logits + bias, NEG_INF)
        es, invs = [], []
        for half in range(2):
            lg = logits[:, half * nk:(half + 1) * nk]
            sink = jnp.full((rows, 1), sink_ref[2 * p0 + half], F32)
            for j in range(1, npair):
                sink = jnp.where(row_pair >= j, sink_ref[2 * (p0 + j) + half], sink)
            m = jnp.maximum(jnp.max(lg, axis=-1, keepdims=True), sink)
            e = jnp.exp(lg - m)
            den = jnp.sum(e, axis=-1, keepdims=True) + jnp.exp(sink - m)
            es.append(e.astype(BF16))
            invs.append(1.0 / den)
        o4 = _dot(jnp.concatenate(es, axis=1), v_diag)
        o4 = (o4 * jnp.where(first_o, invs[0], invs[1])).astype(o_ref.dtype)
        for j in range(npair):
            o_ref[:, pair_cols[j]] = o4[j * A_BLOCK:(j + 1) * A_BLOCK, :]


def _swa_attention(qkv, sinks, bias, window):
    s = qkv.shape[0]
    nq = A_HEADS * A_HEAD_DIM
    nkv = 2 * A_KV_HEADS * A_HEAD_DIM
    kv_blk = nq // nkv
    return pl.pallas_call(
        _swa_kernel,
        grid=(s // A_BLOCK,),
        in_specs=[pl.BlockSpec(memory_space=pltpu.SMEM),
                  pl.BlockSpec((A_BLOCK, nq), lambda n: (n, 0)),
                  pl.BlockSpec((A_BLOCK, nkv), lambda n: (jnp.maximum(n - 1, 0), kv_blk)),
                  pl.BlockSpec((A_BLOCK, nkv), lambda n: (n, kv_blk)),
                  pl.BlockSpec((A_HEADS // 2, A_BLOCK, 4 * A_BLOCK), lambda n: (0, 0, 0)),
                  pl.BlockSpec(window.shape, lambda n: (0, 0))],
        out_specs=pl.BlockSpec((A_BLOCK, nq), lambda n: (n, 0)),
        out_shape=jax.ShapeDtypeStruct((s, nq), BF16),
        compiler_params=_params(("arbitrary",), VMEM_MIB["swa"]),
        name="swa_attention",
    )(sinks.astype(F32), qkv, qkv, qkv, bias, window)


def _swa_layer(h, x, g_pre, w_qkv, sinks, w_o, rel_table, g_post, g_next):
    if h is None:
        qkv = _norm_proj(x, g_pre, w_qkv.astype(BF16), name="swa_qkv")
    else:
        qkv = _proj(h, w_qkv.astype(BF16), BF16, tn=w_qkv.shape[1] // 2, name="swa_qkv")
    bucket, window = _swa_geometry()
    bias = _swa_bias(rel_table, bucket)
    o = _swa_attention(qkv, sinks, bias, jnp.asarray(window))
    return _out_proj(o, w_o.astype(BF16), jnp.zeros((D_MODEL,), F32), x, g_post, g_next, name="swa_out")


def _gla_gate_kernel(h_ref, w1_ref, w2_ref, b_ref, o_ref):
    t = _dot(h_ref[...], w1_ref[...]).astype(BF16)
    gk = _dot(t, w2_ref[...]) + b_ref[...]
    log_sig = -(jnp.maximum(-gk, 0.0) + jnp.log1p(jnp.exp(-jnp.abs(gk))))
    o_ref[...] = log_sig / B_GATE_TAU


def _gla_gate(h, w1, w2, b, tm=512):
    s, d = h.shape
    n = w2.shape[1]
    w1p = jnp.zeros((d, LANES), BF16).at[:, :B_GATE_RANK].set(w1.astype(BF16))
    w2p = jnp.zeros((LANES, n), BF16).at[:B_GATE_RANK, :].set(w2.astype(BF16))
    return pl.pallas_call(
        _gla_gate_kernel,
        grid=(s // tm,),
        in_specs=[pl.BlockSpec((tm, d), lambda i: (i, 0)), pl.BlockSpec((d, LANES), lambda i: (0, 0)),
                  pl.BlockSpec((LANES, n), lambda i: (0, 0)), pl.BlockSpec((1, n), lambda i: (0, 0))],
        out_specs=pl.BlockSpec((tm, n), lambda i: (i, 0)),
        out_shape=jax.ShapeDtypeStruct((s, n), F32),
        compiler_params=_params(("arbitrary",), VMEM_MIB["gla_gate"]),
        name="gla_gate",
    )(h, w1p, w2p, _row(b))


def _gla_kernel(qkvr_ref, la_ref, onorm_ref, o_ref, st_ref, *, n_sub):
    @pl.when(pl.program_id(0) == 0)
    def _():
        st_ref[...] = jnp.zeros_like(st_ref)

    c = B_CHUNK
    dk_all = B_HEADS * B_KEY_DIM
    dv_all = B_HEADS * B_VAL_DIM
    causal = (lax.broadcasted_iota(jnp.int32, (c, c), 0) >= lax.broadcasted_iota(jnp.int32, (c, c), 1))
    tri = causal.astype(BF16)

    def chunk(ci, carry):
        rows = pl.ds(pl.multiple_of(ci * c, c), c)
        la = la_ref[rows, :]
        la_hi = la.astype(BF16)
        la_lo = (la - la_hi.astype(F32)).astype(BF16)
        cum = _dot(tri, la_hi) + _dot(tri, la_lo)
        last = cum[c - 1:c, :]
        q = qkvr_ref[rows, 0:dk_all].astype(F32) * (B_KEY_DIM ** -0.5)
        k = qkvr_ref[rows, dk_all:2 * dk_all].astype(F32)
        q_dec = (q * jnp.exp(cum)).astype(BF16)
        k_inv = (k * jnp.exp(-cum)).astype(BF16)
        k_end = (k * jnp.exp(last - cum)).astype(BF16)
        decay = jnp.exp(last)
        for hh in range(B_HEADS):
            ks = slice(hh * B_KEY_DIM, (hh + 1) * B_KEY_DIM)
            v = qkvr_ref[rows, 2 * dk_all + hh * B_VAL_DIM:2 * dk_all + (hh + 1) * B_VAL_DIM]
            r = qkvr_ref[rows, 2 * dk_all + dv_all + hh * B_VAL_DIM:
                         2 * dk_all + dv_all + (hh + 1) * B_VAL_DIM].astype(F32)
            att = jnp.where(causal, _dot_nt(q_dec[:, ks], k_inv[:, ks]), 0.0).astype(BF16)
            st = st_ref[hh]
            o = _dot(att, v) + _dot_nt(q_dec[:, ks], st.astype(BF16))
            st_ref[hh] = st * decay[:, ks] + _dot_tn(v, k_end[:, ks])
            o = _rms(o) * onorm_ref[...]
            o_ref[rows, hh * B_VAL_DIM:(hh + 1) * B_VAL_DIM] = (o * _silu(r)).astype(o_ref.dtype)
        return carry

    lax.fori_loop(0, n_sub, chunk, 0, unroll=True)


def _gla_core(qkvr, log_a, o_norm, tb=512):
    s = qkvr.shape[0]
    dv_all = B_HEADS * B_VAL_DIM
    return pl.pallas_call(
        functools.partial(_gla_kernel, n_sub=tb // B_CHUNK),
        grid=(s // tb,),
        in_specs=[pl.BlockSpec((tb, qkvr.shape[1]), lambda i: (i, 0)),
                  pl.BlockSpec((tb, log_a.shape[1]), lambda i: (i, 0)),
                  pl.BlockSpec((1, B_VAL_DIM), lambda i: (0, 0))],
        out_specs=pl.BlockSpec((tb, dv_all), lambda i: (i, 0)),
        out_shape=jax.ShapeDtypeStruct((s, dv_all), BF16),
        scratch_shapes=[pltpu.VMEM((B_HEADS, B_VAL_DIM, B_KEY_DIM), F32)],
        compiler_params=_params(("arbitrary",), VMEM_MIB["gla"]),
        name="gla_core",
    )(qkvr, log_a, _row(o_norm))


def _gla_layer(h, x, w_qkvr, w_g1, w_g2, g_bias, o_norm, w_o, g_post, g_next):
    qkvr = _proj(h, w_qkvr.astype(BF16), BF16, tn=2048, name="gla_qkvr")
    log_a = _gla_gate(h, w_g1, w_g2, g_bias)
    o = _gla_core(qkvr, log_a, o_norm)
    return _out_proj(o, w_o.astype(BF16), jnp.zeros((D_MODEL,), F32), x, g_post, g_next, name="gla_out")


def _dwconv_ln_kernel(u_ref, uprev_ref, w_ref, b_ref, lng_ref, lnb_ref, o_ref, ext_ref, z_ref, *, tm, rc, cw):
    i = pl.program_id(0)
    ext_ref[0:C_HALO, :] = jnp.where(i > 0, uprev_ref[...], 0.0)
    ext_ref[C_HALO:C_HALO + tm, :] = u_ref[...]
    d = u_ref.shape[1]
    first = C_HALO - (C_KERNEL - 1)
    sub = SUBLANES
    for r0 in range(0, tm, rc):
        for c0 in range(0, d, cw):
            cols = slice(c0, c0 + cw)
            acc = jnp.zeros((rc, cw), F32) + b_ref[:, cols]
            for b in range(sub):
                rows = rc if b == 0 else rc + sub
                part = None
                for s in range(first, first + C_KERNEL):
                    if s % sub != b:
                        continue
                    term = w_ref[s - first:s - first + 1, cols] * ext_ref[r0 + s - b:r0 + s - b + rows, cols]
                    part = term if part is None else part + term
                if b:
                    part = pltpu.roll(part, rows - b, axis=0)[0:rc, :]
                acc = acc + part
            z_ref[r0:r0 + rc, cols] = acc
    z = z_ref[...]
    zc = z - jnp.mean(z, axis=-1, keepdims=True)
    zn = zc * lax.rsqrt(jnp.mean(zc * zc, axis=-1, keepdims=True) + EPS) * lng_ref[...] + lnb_ref[...]
    o_ref[...] = _silu(zn).astype(o_ref.dtype)


def _dwconv_ln(u, w_dw, b_dw, ln_g, ln_b, tm=128, rc=128, cw=128):
    s, d = u.shape
    row = pl.BlockSpec((1, d), lambda i: (0, 0))
    halo_blocks = tm // C_HALO
    return pl.pallas_call(
        functools.partial(_dwconv_ln_kernel, tm=tm, rc=rc, cw=cw),
        grid=(s // tm,),
        in_specs=[pl.BlockSpec((tm, d), lambda i: (i, 0)),
                  pl.BlockSpec((C_HALO, d), lambda i: (jnp.maximum(i * halo_blocks - 1, 0), 0)),
                  pl.BlockSpec((C_KERNEL, d), lambda i: (0, 0)), row, row, row],
        out_specs=pl.BlockSpec((tm, d), lambda i: (i, 0)),
        out_shape=jax.ShapeDtypeStruct((s, d), BF16),
        scratch_shapes=[pltpu.VMEM((C_HALO + tm, d), F32), pltpu.VMEM((tm, d), F32)],
        compiler_params=_params(("arbitrary",), VMEM_MIB["dwconv"]),
        name="dwconv_ln",
    )(u, u, w_dw.astype(F32), _row(b_dw), _row(ln_g), _row(ln_b))


def _conformer_layer(h, x, w_pw1, b_pw1, w_dw, b_dw, ln_g, ln_b, w_pw2, b_pw2, g_post, g_next):
    u = _proj_glu(h, w_pw1.astype(BF16), b_pw1, tn=1024, name="conf_pw1")
    z = _dwconv_ln(u, w_dw, b_dw, ln_g, ln_b)
    return _out_proj(z, w_pw2.astype(BF16), b_pw2, x, g_post, g_next, name="conf_out")


def _sgu_kernel(u_ref, v_ref, ws_ref, bs_ref, lng_ref, lnb_ref, wout_ref, x_ref, gp_ref, gn_ref,
                xo_ref, ho_ref, p_ref, *, tm):
    t = D_CHUNK
    gw = D_HALF // D_GROUPS
    tril = lax.broadcasted_iota(jnp.int32, (t, t), 0) >= lax.broadcasted_iota(jnp.int32, (t, t), 1)
    for c0 in range(0, tm, t):
        rs = slice(c0, c0 + t)
        v = v_ref[rs, :]
        vc = v - jnp.mean(v, axis=-1, keepdims=True)
        vn = (vc * lax.rsqrt(jnp.mean(vc * vc, axis=-1, keepdims=True) + EPS) * lng_ref[...]
              + lnb_ref[...]).astype(BF16)
        for g in range(D_GROUPS):
            w = jnp.where(tril, ws_ref[g], jnp.zeros((t, t), BF16))
            cols = slice(g * gw, (g + 1) * gw)
            sv = _dot(w, vn[:, cols]) + bs_ref[g]
            p_ref[rs, cols] = (u_ref[rs, cols].astype(F32) * sv).astype(BF16)
        y = _dot(p_ref[rs, :], wout_ref[...])
        xn, hn = _residual_norm(y, x_ref[rs, :], gp_ref[...], gn_ref[...])
        xo_ref[rs, :] = xn
        ho_ref[rs, :] = hn


def _sgu(u, v, w_s, b_s, ln_g, ln_b, w_out, x, g_post, g_next, tm=256):
    s, dh = u.shape
    d = w_out.shape[1]
    row = pl.BlockSpec((1, d), lambda i: (0, 0))
    rowh = pl.BlockSpec((1, dh), lambda i: (0, 0))
    return pl.pallas_call(
        functools.partial(_sgu_kernel, tm=tm),
        grid=(s // tm,),
        in_specs=[pl.BlockSpec((tm, dh), lambda i: (i, 0)), pl.BlockSpec((tm, dh), lambda i: (i, 0)),
                  pl.BlockSpec((D_GROUPS, D_CHUNK, D_CHUNK), lambda i: (0, 0, 0)),
                  pl.BlockSpec((D_GROUPS, D_CHUNK, 1), lambda i: (0, 0, 0)),
                  rowh, rowh,
                  pl.BlockSpec((dh, d), lambda i: (0, 0), pipeline_mode=pl.Buffered(1)),
                  pl.BlockSpec((tm, d), lambda i: (i, 0)), row, row],
        out_specs=[pl.BlockSpec((tm, d), lambda i: (i, 0)), pl.BlockSpec((tm, d), lambda i: (i, 0))],
        out_shape=[jax.ShapeDtypeStruct((s, d), F32), jax.ShapeDtypeStruct((s, d), BF16)],
        scratch_shapes=[pltpu.VMEM((tm, dh), BF16)],
        compiler_params=_params(("arbitrary",), VMEM_MIB["sgu"]),
        name="sgu",
    )(u, v, w_s.astype(BF16), b_s.astype(F32)[:, :, None], _row(ln_g), _row(ln_b), w_out.astype(BF16),
      x, _row(g_post), _row(g_next))


def _gmlp_layer(h, x, w_in, b_in, ln_g, ln_b, w_s, b_s, w_out, g_post, g_next):
    w_in = w_in.astype(BF16)
    u = _proj_gelu(h, w_in[:, :D_HALF], b_in[:D_HALF], BF16, tn=2048, name="gmlp_in_u")
    v = _proj_gelu(h, w_in[:, D_HALF:], b_in[D_HALF:], F32, tn=2048, name="gmlp_in_v")
    return _sgu(u, v, w_s, b_s, ln_g, ln_b, w_out, x, g_post, g_next)


def _mem_kv_kernel(mem_ref, g_ref, w_ref, o_ref):
    mem_n = (_rms(mem_ref[...]) * g_ref[...]).astype(BF16)
    o_ref[...] = _dot(mem_n, w_ref[...]).astype(o_ref.dtype)


def _mem_kv(mem, g_all, w_kv_all):
    m, d = mem.shape
    nl, _, n = w_kv_all.shape
    tn = n // 2
    return pl.pallas_call(
        _mem_kv_kernel,
        grid=(nl, n // tn),
        in_specs=[pl.BlockSpec((m, d), lambda l, j: (0, 0)),
                  pl.BlockSpec((None, 1, d), lambda l, j: (l, 0, 0)),
                  pl.BlockSpec((None, d, tn), lambda l, j: (l, 0, j))],
        out_specs=pl.BlockSpec((None, m, tn), lambda l, j: (l, 0, j)),
        out_shape=jax.ShapeDtypeStruct((nl, m, n), BF16),
        compiler_params=_params(("arbitrary", "arbitrary"), VMEM_MIB["mem_kv"]),
        name="mem_kv",
    )(mem, g_all.reshape(nl, 1, d).astype(F32), w_kv_all.astype(BF16))


def _xattn_kernel(h_ref, wq_ref, kv_ref, wo_ref, x_ref, gp_ref, gn_ref, xo_ref, ho_ref, *, n_split):
    hd = X_HEAD_DIM
    nq = X_HEADS * hd
    rows = h_ref.shape[0] // n_split
    for r0 in range(0, h_ref.shape[0], rows):
        rs = slice(r0, r0 + rows)
        q = (_dot(h_ref[rs, :], wq_ref[...]) * (hd ** -0.5)).astype(BF16)
        outs = []
        for hh in range(X_HEADS):
            cols = slice(hh * hd, (hh + 1) * hd)
            logits = _dot_nt(q[:, cols], kv_ref[:, cols])
            e = jnp.exp(logits - jnp.max(logits, axis=-1, keepdims=True))
            inv = 1.0 / jnp.sum(e, axis=-1, keepdims=True)
            o = _dot(e.astype(BF16), kv_ref[:, nq + hh * hd:nq + (hh + 1) * hd])
            outs.append((o * inv).astype(BF16))
        y = _dot(jnp.concatenate(outs, axis=1), wo_ref[...])
        xn, hn = _residual_norm(y, x_ref[rs, :], gp_ref[...], gn_ref[...])
        xo_ref[rs, :] = xn
        ho_ref[rs, :] = hn


def _xattn(h, x, w_q, kv, w_o, g_post, g_next, tm=512):
    s, d = h.shape
    nq = w_q.shape[1]
    row = pl.BlockSpec((1, d), lambda i: (0, 0))
    return pl.pallas_call(
        functools.partial(_xattn_kernel, n_split=1),
        grid=(s // tm,),
        in_specs=[pl.BlockSpec((tm, d), lambda i: (i, 0)), pl.BlockSpec((d, nq), lambda i: (0, 0)),
                  pl.BlockSpec(kv.shape, lambda i: (0, 0)), pl.BlockSpec((nq, d), lambda i: (0, 0)),
                  pl.BlockSpec((tm, d), lambda i: (i, 0)), row, row],
        out_specs=[pl.BlockSpec((tm, d), lambda i: (i, 0)), pl.BlockSpec((tm, d), lambda i: (i, 0))],
        out_shape=[jax.ShapeDtypeStruct((s, d), F32), jax.ShapeDtypeStruct((s, d), BF16)],
        compiler_params=_params(("arbitrary",), VMEM_MIB["xattn"]),
        name="xattn",
    )(h, w_q.astype(BF16), kv, w_o.astype(BF16), x, _row(g_post), _row(g_next))


def _ffn_kernel(h_ref, wg_ref, wu_ref, wc_ref, bc_ref, wd_ref, x_ref, gp_ref, *rest, tm, emit_h):
    if emit_h:
        gn_ref, xo_ref, ho_ref, halo_ref, ext_ref, xs_ref = rest
    else:
        xo_ref, halo_ref, ext_ref, xs_ref = rest
    i = pl.program_id(0)
    f = pl.program_id(1)
    nf = xs_ref.shape[0]
    xs_ref[f] = x_ref[...]

    @pl.when(i == 0)
    def _():
        halo_ref[f] = jnp.zeros(halo_ref.shape[1:], F32)

    @pl.when(f == 0)
    def _():
        xo_ref[...] = jnp.zeros_like(xo_ref)

    h = h_ref[...]
    gate = _dot(h, wg_ref[...])
    up = _dot(h, wu_ref[...])
    ext_ref[0:FFN_HALO, :] = halo_ref[f]
    ext_ref[FFN_HALO:FFN_HALO + tm, :] = gate
    halo_ref[f] = gate[tm - FFN_HALO:tm, :]
    conv = (wc_ref[2:3, :] * gate
            + wc_ref[1:2, :] * ext_ref[FFN_HALO - 1:FFN_HALO - 1 + tm, :]
            + wc_ref[0:1, :] * ext_ref[FFN_HALO - 2:FFN_HALO - 2 + tm, :]
            + bc_ref[...])
    act = (_gelu_tanh(conv) * up).astype(BF16)
    xo_ref[...] += _dot(act, wd_ref[...])

    @pl.when(f == pl.num_programs(1) - 1)
    def _():
        x = jnp.concatenate([xs_ref[j] for j in range(nf)], axis=1)
        xn, hn = _residual_norm(xo_ref[...], x, gp_ref[...], gn_ref[...] if emit_h else None)
        xo_ref[...] = xn
        if emit_h:
            ho_ref[...] = hn


def _ffn(h, x, layer, w_gate_up, w_conv, b_conv, w_down, g_post, g_next, tm=1024, fc=FFN_CHUNK):
    s, d = h.shape
    ff = w_down.shape[1]
    nf = ff // fc
    emit_h = g_next is not None
    row = pl.BlockSpec((1, d), lambda i, f: (0, 0))
    tile = pl.BlockSpec((tm, d), lambda i, f: (i, 0))
    x_tile = pl.BlockSpec((tm, d // nf), lambda i, f: (i, f))
    in_specs = [tile,
                pl.BlockSpec((None, d, fc), lambda i, f: (layer, 0, f)),
                pl.BlockSpec((None, d, fc), lambda i, f: (layer, 0, f + nf)),
                pl.BlockSpec((FFN_KERNEL, fc), lambda i, f: (0, f)),
                pl.BlockSpec((1, fc), lambda i, f: (0, f)),
                pl.BlockSpec((None, fc, d), lambda i, f: (layer, f, 0)),
                x_tile, row]
    args = [h, w_gate_up, w_gate_up, w_conv.astype(F32), _row(b_conv), w_down, x, _row(g_post)]
    out_specs = [tile]
    out_shape = [jax.ShapeDtypeStruct((s, d), F32)]
    if emit_h:
        in_specs.append(row)
        args.append(_row(g_next))
        out_specs.append(tile)
        out_shape.append(jax.ShapeDtypeStruct((s, d), BF16))
    outs = pl.pallas_call(
        functools.partial(_ffn_kernel, tm=tm, emit_h=emit_h),
        grid=(s // tm, nf),
        in_specs=in_specs,
        out_specs=out_specs,
        out_shape=out_shape,
        scratch_shapes=[pltpu.VMEM((nf, FFN_HALO, fc), F32), pltpu.VMEM((FFN_HALO + tm, fc), F32),
                        pltpu.VMEM((nf, tm, d // nf), F32)],
        compiler_params=_params(("arbitrary", "arbitrary"), VMEM_MIB["ffn"]),
        name="ffn",
    )(*args)
    return (outs[0], outs[1]) if emit_h else (outs[0], None)


def kernel(x, mem, norm_mix_pre, norm_mix_post, norm_mem, norm_xattn_pre, norm_xattn_post, norm_ffn_pre, norm_ffn_post, rel_bias_table, a_w_qkv, a_sinks, a_w_o, b_w_qkvr, b_w_gate1, b_w_gate2, b_gate_bias, b_o_norm, b_w_o, c_w_pw1, c_b_pw1, c_w_dw, c_b_dw, c_ln_g, c_ln_b, c_w_pw2, c_b_pw2, d_w_in, d_b_in, d_ln_g, d_ln_b, d_w_s, d_b_s, d_w_out, x_w_q, x_w_kv, x_w_o, f_w_gate_up, f_w_conv, f_b_conv, f_w_down):
    assert x.shape[0] == 1 and mem.shape[0] == 1
    xs = x[0]
    mem2 = mem[0]
    w_gate_up = f_w_gate_up.astype(BF16)
    w_down = f_w_down.astype(BF16)
    kv_all = _mem_kv(mem2, norm_mem, x_w_kv)
    h = None
    for i in range(DEPTH):
        kind, j = i % 4, i // 4
        g_post, g_next = norm_mix_post[i], norm_xattn_pre[i]
        if kind == 0:
            xs, h = _swa_layer(h, xs, norm_mix_pre[i], a_w_qkv[j], a_sinks[j], a_w_o[j], rel_bias_table,
                               g_post, g_next)
        elif kind == 1:
            xs, h = _gla_layer(h, xs, b_w_qkvr[j], b_w_gate1[j], b_w_gate2[j], b_gate_bias[j], b_o_norm[j],
                               b_w_o[j], g_post, g_next)
        elif kind == 2:
            xs, h = _conformer_layer(h, xs, c_w_pw1[j], c_b_pw1[j], c_w_dw[j], c_b_dw[j], c_ln_g[j],
                                     c_ln_b[j], c_w_pw2[j], c_b_pw2[j], g_post, g_next)
        else:
            xs, h = _gmlp_layer(h, xs, d_w_in[j], d_b_in[j], d_ln_g[j], d_ln_b[j], d_w_s[j], d_b_s[j],
                                d_w_out[j], g_post, g_next)
        kv = kv_all[i]
        xs, h = _xattn(h, xs, x_w_q[i], kv, x_w_o[i], norm_xattn_post[i], norm_ffn_pre[i])
        g_next = norm_mix_pre[i + 1] if i + 1 < DEPTH else None
        xs, h = _ffn(h, xs, i, w_gate_up, f_w_conv[i], f_b_conv[i], w_down, norm_ffn_post[i], g_next)
    return xs[None]
```
